```python
import jax, jax.numpy as jnp
from jax import lax
import numpy as np

D_MODEL = 2048
BATCH = 2
SEQ = 8192
DEPTH = 1
DEC_BATCH = 8
DEC_SEQ = 64
PAST_LEN = 4096

CHUNK = 64
MIX_WIDTH = D_MODEL
POOL_WIDTH = MIX_WIDTH // 2
POOL_WINDOWS = (2, 4, 8, 16)
N_POOL_GROUPS = len(POOL_WINDOWS)
POOL_GROUP = POOL_WIDTH // N_POOL_GROUPS
POOL_BUF = max(POOL_WINDOWS) - 1
ATTN_WIDTH = MIX_WIDTH - POOL_WIDTH
HEAD_DIM = 128
N_HEADS = ATTN_WIDTH // HEAD_DIM
Q_BLOCK = 128
IN_WIDTH = POOL_WIDTH + 3 * ATTN_WIDTH + N_HEADS
N_EXPERTS = 64
TOP_K = 8
D_EXPERT = 512
D_SHARED = 512
ROUTED_SCALE = 2.5
EXPERT_BLOCK = 128
EPS = 1e-6
NEG = -1e30

kernel_name = 'hybrid_pool_fox_moe_step'


def rmsnorm(x, g):
    xf = x.astype(jnp.float32)
    y = xf * lax.rsqrt(jnp.mean(xf * xf, axis=-1, keepdims=True) + EPS) * g.astype(jnp.float32)
    return y.astype(x.dtype)


def mixer_inputs(x, g_mix, w_in, b_f, g_q, g_k):
    B, L, _ = x.shape
    h = rmsnorm(x, g_mix)
    z = h @ w_in
    a = POOL_WIDTH
    u = z[..., :a]
    q = z[..., a:a + ATTN_WIDTH].reshape(B, L, N_HEADS, HEAD_DIM)
    k = z[..., a + ATTN_WIDTH:a + 2 * ATTN_WIDTH].reshape(B, L, N_HEADS, HEAD_DIM)
    v = z[..., a + 2 * ATTN_WIDTH:a + 3 * ATTN_WIDTH].reshape(B, L, N_HEADS, HEAD_DIM)
    logf = jax.nn.log_sigmoid((z[..., a + 3 * ATTN_WIDTH:] + b_f).astype(jnp.float32))
    return u, rmsnorm(q, g_q), rmsnorm(k, g_k), v, logf


def pool_mix(u, prefix, start_pos, w_pool, pool_scale):
    B, L, _ = u.shape
    xp = jnp.concatenate([prefix.astype(u.dtype), u], axis=1).astype(jnp.float32)
    cs = jnp.cumsum(jnp.pad(xp, ((0, 0), (1, 0), (0, 0))), axis=1)
    off = POOL_BUF + 1
    pos = start_pos + jnp.arange(L)
    means = []
    for gi, w in enumerate(POOL_WINDOWS):
        sl = slice(gi * POOL_GROUP, (gi + 1) * POOL_GROUP)
        s = cs[:, off:off + L, sl] - cs[:, off - w:off - w + L, sl]
        cnt = jnp.minimum(w, pos + 1).astype(jnp.float32)
        means.append(s / cnt[None, :, None])
    d = (jnp.concatenate(means, axis=-1) - u.astype(jnp.float32)).astype(u.dtype)
    d = d.reshape(B, L, N_POOL_GROUPS, POOL_GROUP)
    out = jnp.einsum('blgc,gcd->blgd', d, w_pool).reshape(B, L, POOL_WIDTH) * pool_scale
    return out, xp[:, -POOL_BUF:].astype(u.dtype)


def attend(q, k, v, cq, ck, qpos, kpos):
    s = jnp.einsum('bqhd,bkhd->bhqk', q, k).astype(jnp.float32) * (HEAD_DIM ** -0.5)
    s = s + jnp.transpose(cq, (0, 2, 1))[:, :, :, None] - jnp.transpose(ck, (0, 2, 1))[:, :, None, :]
    mask = qpos[:, None] >= kpos[None, :]
    s = jnp.where(mask[None, None], s, NEG)
    p = jax.nn.softmax(s, axis=-1).astype(v.dtype)
    return jnp.einsum('bhqk,bkhd->bqhd', p, v)


def prompt_attention(q, k, v, logf):
    B, S = q.shape[:2]
    c = jnp.cumsum(logf.astype(jnp.float32), axis=1)
    pos = jnp.arange(S)
    nqb = S // Q_BLOCK
    qb = q.reshape(B, nqb, Q_BLOCK, N_HEADS, HEAD_DIM).swapaxes(0, 1)
    cb = c.reshape(B, nqb, Q_BLOCK, N_HEADS).swapaxes(0, 1)
    pb = pos.reshape(nqb, Q_BLOCK)
    out = lax.map(lambda xs: attend(xs[0], k, v, xs[1], c, xs[2], pos), (qb, cb, pb))
    return out.swapaxes(0, 1).reshape(B, S, N_HEADS, HEAD_DIM)


def sample_attention(q, k, v, logf, ck_cache, cv_cache, clogf_cache):
    P = ck_cache.shape[1]
    S = q.shape[1]
    k_all = jnp.concatenate([ck_cache.astype(k.dtype), k], axis=1)
    v_all = jnp.concatenate([cv_cache.astype(v.dtype), v], axis=1)
    c_all = jnp.cumsum(jnp.concatenate([clogf_cache.astype(jnp.float32), logf], axis=1), axis=1)
    kpos = jnp.arange(P + S)
    return attend(q, k_all, v_all, c_all[:, P:], c_all, P + jnp.arange(S), kpos)


def swiglu(h, wg, wu, wd):
    return (jax.nn.silu(h @ wg) * (h @ wu)) @ wd


def moe(h, w_router, router_bias, w_gate, w_up, w_down, ws_gate, ws_up, ws_down):
    T, D = h.shape
    scores = jax.nn.sigmoid((h @ w_router).astype(jnp.float32))
    _, idx = lax.top_k(scores + router_bias.astype(jnp.float32), TOP_K)
    g = jnp.take_along_axis(scores, idx, axis=1)
    g = g / jnp.sum(g, axis=1, keepdims=True) * ROUTED_SCALE
    A = T * TOP_K
    flat_e = idx.reshape(A)
    flat_t = jnp.repeat(jnp.arange(T, dtype=jnp.int32), TOP_K)
    flat_g = g.reshape(A)
    order = jnp.argsort(flat_e, stable=True)
    se, st, sg = flat_e[order], flat_t[order], flat_g[order]
    counts = jnp.bincount(flat_e, length=N_EXPERTS)
    starts = jnp.cumsum(counts) - counts
    pcounts = ((counts + EXPERT_BLOCK - 1) // EXPERT_BLOCK) * EXPERT_BLOCK
    pends = jnp.cumsum(pcounts)
    pstarts = pends - pcounts
    dest = pstarts[se] + jnp.arange(A) - starts[se]
    NB = -(-A // EXPERT_BLOCK) + N_EXPERTS
    buf_t = jnp.full((NB * EXPERT_BLOCK,), T, jnp.int32).at[dest].set(st)
    buf_g = jnp.zeros((NB * EXPERT_BLOCK,), jnp.float32).at[dest].set(sg)
    blk_e = jnp.minimum(jnp.searchsorted(pends, jnp.arange(NB) * EXPERT_BLOCK, side='right'), N_EXPERTS - 1)
    h_pad = jnp.concatenate([h, jnp.zeros((1, D), h.dtype)], axis=0)

    def body(out, xs):
        tb, gb, e = xs
        y = swiglu(h_pad[tb], w_gate[e], w_up[e], w_down[e]).astype(jnp.float32)
        return out.at[tb].add(y * gb[:, None]), None

    out, _ = lax.scan(body, jnp.zeros((T + 1, D), jnp.float32),
                      (buf_t.reshape(NB, EXPERT_BLOCK), buf_g.reshape(NB, EXPERT_BLOCK), blk_e))
    return (out[:T] + swiglu(h, ws_gate, ws_up, ws_down).astype(jnp.float32)).astype(h.dtype)


def finish_layer(x, pool_out, attn_out, w_o, g_ffn, w_router, router_bias, w_gate, w_up, w_down,
                 ws_gate, ws_up, ws_down):
    B, L, D = x.shape
    mix = jnp.concatenate([pool_out, attn_out.reshape(B, L, ATTN_WIDTH).astype(pool_out.dtype)], axis=-1)
    x = x + mix @ w_o
    h = rmsnorm(x, g_ffn).reshape(B * L, D)
    return x + moe(h, w_router, router_bias, w_gate, w_up, w_down, ws_gate, ws_up, ws_down).reshape(B, L, D)


def setup_inputs(seed: int = 0) -> dict:
    key = jax.random.key(seed)
    ks = jax.random.split(key, 24)
    f32 = jnp.float32
    nrm = lambda k, shape, scale: jax.random.normal(k, shape, f32) * scale
    return {
        'x_prompt': nrm(ks[0], (BATCH, SEQ, D_MODEL), 1.0),
        'x_sample': nrm(ks[1], (DEC_BATCH, DEC_SEQ, D_MODEL), 1.0),
        'cache_k': nrm(ks[2], (DEPTH, DEC_BATCH, PAST_LEN, N_HEADS, HEAD_DIM), 1.0),
        'cache_v': nrm(ks[3], (DEPTH, DEC_BATCH, PAST_LEN, N_HEADS, HEAD_DIM), 1.0),
        'cache_logf': jax.nn.log_sigmoid(2.0 + nrm(ks[4], (DEPTH, DEC_BATCH, PAST_LEN, N_HEADS), 1.0)),
        'cache_pool': nrm(ks[5], (DEPTH, DEC_BATCH, POOL_BUF, POOL_WIDTH), 1.0),
        'g_mix': 1.0 + nrm(ks[6], (DEPTH, D_MODEL), 0.01),
        'w_in': nrm(ks[7], (DEPTH, D_MODEL, IN_WIDTH), D_MODEL ** -0.5),
        'b_f': 2.0 + nrm(ks[8], (DEPTH, N_HEADS), 0.5),
        'g_q': 1.0 + nrm(ks[9], (DEPTH, HEAD_DIM), 0.01),
        'g_k': 1.0 + nrm(ks[10], (DEPTH, HEAD_DIM), 0.01),
        'w_pool': nrm(ks[11], (DEPTH, N_POOL_GROUPS, POOL_GROUP, POOL_GROUP), POOL_GROUP ** -0.5),
        'pool_scale': 1.0 + nrm(ks[12], (DEPTH, POOL_WIDTH), 0.01),
        'w_o': nrm(ks[13], (DEPTH, MIX_WIDTH, D_MODEL), MIX_WIDTH ** -0.5),
        'g_ffn': 1.0 + nrm(ks[14], (DEPTH, D_MODEL), 0.01),
        'w_router': nrm(ks[15], (DEPTH, D_MODEL, N_EXPERTS), D_MODEL ** -0.5),
        'router_bias': nrm(ks[16], (DEPTH, N_EXPERTS), 0.01),
        'w_gate': nrm(ks[17], (DEPTH, N_EXPERTS, D_MODEL, D_EXPERT), D_MODEL ** -0.5),
        'w_up': nrm(ks[18], (DEPTH, N_EXPERTS, D_MODEL, D_EXPERT), D_MODEL ** -0.5),
        'w_down': nrm(ks[19], (DEPTH, N_EXPERTS, D_EXPERT, D_MODEL), D_EXPERT ** -0.5),
        'ws_gate': nrm(ks[20], (DEPTH, D_MODEL, D_SHARED), D_MODEL ** -0.5),
        'ws_up': nrm(ks[21], (DEPTH, D_MODEL, D_SHARED), D_MODEL ** -0.5),
        'ws_down': nrm(ks[22], (DEPTH, D_SHARED, D_MODEL), D_SHARED ** -0.5),
    }


def reference(x_prompt, x_sample, cache_k, cache_v, cache_logf, cache_pool, g_mix, w_in, b_f, g_q, g_k,
              w_pool, pool_scale, w_o, g_ffn, w_router, router_bias, w_gate, w_up, w_down,
              ws_gate, ws_up, ws_down):
    xp, xs = x_prompt, x_sample
    past = cache_k.shape[2]
    kp, vp, fp, pp, kd, vd, fd, pd = [], [], [], [], [], [], [], []
    for l in range(DEPTH):
        ffn_w = (w_o[l], g_ffn[l], w_router[l], router_bias[l], w_gate[l], w_up[l], w_down[l],
                 ws_gate[l], ws_up[l], ws_down[l])
        u, q, k, v, logf = mixer_inputs(xp, g_mix[l], w_in[l], b_f[l], g_q[l], g_k[l])
        pool_out, pool_state = pool_mix(u, jnp.zeros((xp.shape[0], POOL_BUF, POOL_WIDTH), u.dtype), 0,
                                        w_pool[l], pool_scale[l])
        attn = prompt_attention(q, k, v, logf)
        xp = finish_layer(xp, pool_out, attn, *ffn_w)
        kp.append(k); vp.append(v); fp.append(logf); pp.append(pool_state)
        u, q, k, v, logf = mixer_inputs(xs, g_mix[l], w_in[l], b_f[l], g_q[l], g_k[l])
        pool_out, pool_state = pool_mix(u, cache_pool[l], past, w_pool[l], pool_scale[l])
        attn = sample_attention(q, k, v, logf, cache_k[l], cache_v[l], cache_logf[l])
        xs = finish_layer(xs, pool_out, attn, *ffn_w)
        kd.append(k); vd.append(v); fd.append(logf); pd.append(pool_state)
    new_k_prompt = jnp.stack(kp, 0)
    new_v_prompt = jnp.stack(vp, 0)
    new_logf_prompt = jnp.stack(fp, 0)
    new_pool_prompt = jnp.stack(pp, 0)
    new_k_sample = jnp.stack(kd, 0)
    new_v_sample = jnp.stack(vd, 0)
    new_logf_sample = jnp.stack(fd, 0)
    new_pool_sample = jnp.stack(pd, 0)
    return (xp, xs, new_k_prompt, new_v_prompt, new_logf_prompt, new_pool_prompt,
            new_k_sample, new_v_sample, new_logf_sample, new_pool_sample)
```

```python
import functools

import jax
import jax.numpy as jnp
from jax import lax
from jax.experimental import pallas as pl
from jax.experimental.pallas import tpu as pltpu

D_MODEL = 2048
POOL_WIDTH = 1024
ATTN_WIDTH = 1024
HEAD_DIM = 128
N_HEADS = 8
POOL_WINDOWS = (2, 4, 8, 16)
POOL_GROUP = 256
POOL_BUF = 15
POOL_HALO = 16
N_EXPERTS = 64
TOP_K = 8
D_EXPERT = 512
D_SHARED = 512
ROUTED_SCALE = 2.5
EPS = 1e-6
NEG = -1e30

F32 = jnp.float32
BF16 = jnp.bfloat16
U32 = jnp.uint32
HALF = D_MODEL // 2

V7X_VMEM_LIMIT = 56 * 1024 * 1024

TM_PROJ = 256
TL_SCAN = 512
TL_POOL = 512
T_ATTN = 512
TK_CACHE = 1024
TM_ROWS = 128
M_EXPERT = 256


def _params(sem, vmem=V7X_VMEM_LIMIT):
    return pltpu.CompilerParams(dimension_semantics=sem, vmem_limit_bytes=vmem)


def _resident(shape):
    zeros = (0,) * len(shape)
    return pl.BlockSpec(shape, lambda *_: zeros, pipeline_mode=pl.Buffered(1))


def _dot(a, b):
    return jnp.dot(a, b, preferred_element_type=F32)


def _sigmoid(x):
    return 1.0 / (1.0 + jnp.exp(-x))


def _inproj_kernel(x_ref, gmix_ref, w_ref, wf_ref, bf_ref, gq_ref, gk_ref,
                   u_ref, q_ref, k_ref, v_ref, logf_ref):
    x = x_ref[...]
    ms = jnp.mean(x * x, axis=-1, keepdims=True)
    h = (x * lax.rsqrt(ms + EPS) * gmix_ref[...]).astype(BF16)
    p, a = POOL_WIDTH, ATTN_WIDTH
    u_ref[...] = _dot(h, w_ref[:, :p])
    zq = _dot(h, w_ref[:, p:p + a])
    zk = _dot(h, w_ref[:, p + a:p + 2 * a])
    v_ref[...] = _dot(h, w_ref[:, p + 2 * a:p + 3 * a])
    scale = HEAD_DIM ** -0.5
    for hd in range(N_HEADS):
        sl = slice(hd * HEAD_DIM, (hd + 1) * HEAD_DIM)
        qh = zq[:, sl]
        qn = qh * lax.rsqrt(jnp.mean(qh * qh, axis=-1, keepdims=True) + EPS) * gq_ref[...]
        q_ref[:, sl] = (qn * scale).astype(BF16)
        kh = zk[:, sl]
        k_ref[:, sl] = kh * lax.rsqrt(jnp.mean(kh * kh, axis=-1, keepdims=True) + EPS) * gk_ref[...]
    zf = _dot(h, wf_ref[...])[:, :N_HEADS] + bf_ref[...]
    logf_ref[...] = jnp.minimum(zf, 0.0) - jnp.log1p(jnp.exp(-jnp.abs(zf)))


def _inproj(x, g_mix, w_main, w_f, b_f, g_q, g_k):
    t = x.shape[0]
    tm = min(TM_PROJ, t)
    assert t % tm == 0
    row = lambda w: pl.BlockSpec((tm, w), lambda i: (i, 0))
    return pl.pallas_call(
        _inproj_kernel,
        grid=(t // tm,),
        in_specs=[row(D_MODEL), _resident((1, D_MODEL)), _resident(w_main.shape), _resident(w_f.shape),
                  _resident((1, N_HEADS)), _resident((1, HEAD_DIM)), _resident((1, HEAD_DIM))],
        out_specs=[row(POOL_WIDTH), row(ATTN_WIDTH), row(ATTN_WIDTH), row(ATTN_WIDTH), row(N_HEADS)],
        out_shape=[jax.ShapeDtypeStruct((t, POOL_WIDTH), F32),
                   jax.ShapeDtypeStruct((t, ATTN_WIDTH), BF16),
                   jax.ShapeDtypeStruct((t, ATTN_WIDTH), F32),
                   jax.ShapeDtypeStruct((t, ATTN_WIDTH), F32),
                   jax.ShapeDtypeStruct((t, N_HEADS), F32)],
        compiler_params=_params(("arbitrary",)),
        name="inproj",
    )(x, g_mix, w_main, w_f, b_f, g_q, g_k)


def _cumsum_kernel(x_ref, init_ref, o_ref, carry_ref):
    @pl.when(pl.program_id(1) == 0)
    def _():
        carry_ref[...] = init_ref[0]

    x = x_ref[0]
    tl = x.shape[0]
    r = lax.broadcasted_iota(jnp.int32, (tl, tl), 0)
    c = lax.broadcasted_iota(jnp.int32, (tl, tl), 1)
    tri = (r >= c).astype(F32)
    cs = jnp.dot(tri, x, precision=lax.Precision.HIGHEST, preferred_element_type=F32) + carry_ref[...]
    o_ref[0] = cs
    carry_ref[...] = cs[tl - 1:tl, :]


def _cumsum(x, init):
    b, l, h = x.shape
    tl = min(TL_SCAN, l)
    assert l % tl == 0
    return pl.pallas_call(
        _cumsum_kernel,
        grid=(b, l // tl),
        in_specs=[pl.BlockSpec((1, tl, h), lambda bi, li: (bi, li, 0)),
                  pl.BlockSpec((1, 1, h), lambda bi, li: (bi, 0, 0))],
        out_specs=pl.BlockSpec((1, tl, h), lambda bi, li: (bi, li, 0)),
        out_shape=jax.ShapeDtypeStruct((b, l, h), F32),
        scratch_shapes=[pltpu.VMEM((1, h), F32)],
        compiler_params=_params(("arbitrary", "arbitrary")),
        name="logf_cumsum",
    )(x, init)


def _pool_kernel(u_ref, prefix_ref, wp_ref, ps_ref, o_ref, ext_ref, *, start_pos, tl):
    l = pl.program_id(1)

    @pl.when(l == 0)
    def _():
        ext_ref[0:POOL_HALO, :] = prefix_ref[0]

    u = u_ref[0]
    ext_ref[POOL_HALO:POOL_HALO + tl, :] = u
    pos = start_pos + l * tl + lax.broadcasted_iota(jnp.int32, (tl, 1), 0)
    for g, w in enumerate(POOL_WINDOWS):
        cols = slice(g * POOL_GROUP, (g + 1) * POOL_GROUP)
        s = u[:, cols]
        for j in range(1, w):
            s = s + ext_ref[POOL_HALO - j:POOL_HALO - j + tl, cols]
        cnt = jnp.minimum(w, pos + 1).astype(F32)
        d = s / cnt - u[:, cols]
        og = _dot(d.astype(BF16), wp_ref[g]) * ps_ref[:, cols]
        o_ref[0, :, cols] = og.astype(o_ref.dtype)
    ext_ref[0:POOL_HALO, :] = ext_ref[tl:tl + POOL_HALO, :]


def _pool(u, prefix, w_pool, pool_scale, start_pos):
    b, l, p = u.shape
    tl = min(TL_POOL, l)
    assert l % tl == 0 and tl >= POOL_HALO
    return pl.pallas_call(
        functools.partial(_pool_kernel, start_pos=start_pos, tl=tl),
        grid=(b, l // tl),
        in_specs=[pl.BlockSpec((1, tl, p), lambda bi, li: (bi, li, 0)),
                  pl.BlockSpec((1, POOL_HALO, p), lambda bi, li: (bi, 0, 0)),
                  _resident(w_pool.shape), _resident((1, p))],
        out_specs=pl.BlockSpec((1, tl, p), lambda bi, li: (bi, li, 0)),
        out_shape=jax.ShapeDtypeStruct((b, l, p), BF16),
        scratch_shapes=[pltpu.VMEM((tl + POOL_HALO, p), F32)],
        compiler_params=_params(("arbitrary", "arbitrary")),
        name="pool_mix",
    )(u, prefix, w_pool, pool_scale)


def _softmax_step(s, v, m_prev, l_prev, acc_prev):
    m_new = jnp.maximum(m_prev, jnp.max(s, axis=1, keepdims=True))
    alpha = jnp.exp(m_prev - m_new)
    p = jnp.exp(s - m_new)
    l_new = alpha * l_prev + jnp.sum(p, axis=1, keepdims=True)
    acc_new = alpha * acc_prev + _dot(p.astype(BF16), v)
    return m_new, l_new, acc_new


def _qk(q, k):
    return lax.dot_general(q, k, (((1,), (1,)), ((), ())), preferred_element_type=F32)


def _causal(s):
    r = lax.broadcasted_iota(jnp.int32, s.shape, 0)
    c = lax.broadcasted_iota(jnp.int32, s.shape, 1)
    return jnp.where(r >= c, s, NEG)


def _flash_kernel(q_ref, k_ref, v_ref, cq_ref, ck_ref, o_ref, m_sc, l_sc, acc_sc, cq_sc):
    hd = pl.program_id(1)
    i = pl.program_id(2)
    j = pl.program_id(3)

    @pl.when(j == 0)
    def _():
        m_sc[...] = jnp.full(m_sc.shape, NEG, F32)
        l_sc[...] = jnp.zeros(l_sc.shape, F32)
        acc_sc[...] = jnp.zeros(acc_sc.shape, F32)
        cq = cq_ref[0]
        lane = lax.broadcasted_iota(jnp.int32, cq.shape, 1)
        cq_sc[...] = jnp.sum(jnp.where(lane == hd, cq, 0.0), axis=1, keepdims=True)

    def step(masked):
        s = _qk(q_ref[0], k_ref[0].astype(BF16)) + cq_sc[...] - ck_ref[0]
        if masked:
            s = _causal(s)
        m, l, acc = _softmax_step(s, v_ref[0].astype(BF16), m_sc[...], l_sc[...], acc_sc[...])
        m_sc[...] = m
        l_sc[...] = l
        acc_sc[...] = acc

    @pl.when(j < i)
    def _():
        step(False)

    @pl.when(j == i)
    def _():
        step(True)
        o_ref[0] = (acc_sc[...] / l_sc[...]).astype(o_ref.dtype)


def _prompt_attention(q, k, v, c_col, c_row):
    b, s, _ = q.shape
    t = min(T_ATTN, s)
    assert s % t == 0
    n = s // t
    kv_spec = pl.BlockSpec((1, t, HEAD_DIM), lambda bi, hi, i, j: (bi, jnp.minimum(j, i), hi))
    return pl.pallas_call(
        _flash_kernel,
        grid=(b, N_HEADS, n, n),
        in_specs=[pl.BlockSpec((1, t, HEAD_DIM), lambda bi, hi, i, j: (bi, i, hi)),
                  kv_spec, kv_spec,
                  pl.BlockSpec((1, t, N_HEADS), lambda bi, hi, i, j: (bi, i, 0)),
                  pl.BlockSpec((1, 1, t), lambda bi, hi, i, j: (bi * N_HEADS + hi, 0, jnp.minimum(j, i)))],
        out_specs=pl.BlockSpec((1, t, HEAD_DIM), lambda bi, hi, i, j: (bi, i, hi)),
        out_shape=jax.ShapeDtypeStruct((b, s, ATTN_WIDTH), BF16),
        scratch_shapes=[pltpu.VMEM((t, 1), F32), pltpu.VMEM((t, 1), F32),
                        pltpu.VMEM((t, HEAD_DIM), F32), pltpu.VMEM((t, 1), F32)],
        compiler_params=_params(("parallel", "parallel", "parallel", "arbitrary")),
        name="prompt_attention",
    )(q, k, v, c_col, c_row)


def _sample_attn_kernel(q_ref, ck_ref, cv_ref, kn_ref, vn_ref, cq_ref, cc_ref, cn_ref, o_ref,
                        m_sc, l_sc, acc_sc, *, nk):
    j = pl.program_id(1)

    @pl.when(j == 0)
    def _():
        m_sc[...] = jnp.full(m_sc.shape, NEG, F32)
        l_sc[...] = jnp.zeros(l_sc.shape, F32)
        acc_sc[...] = jnp.zeros(acc_sc.shape, F32)

    def step(k_ref, v_ref, crow_ref, masked):
        for hd in range(N_HEADS):
            sl = slice(hd * HEAD_DIM, (hd + 1) * HEAD_DIM)
            s = _qk(q_ref[0, :, sl], k_ref[0, :, sl].astype(BF16))
            s = s + cq_ref[0, :, hd:hd + 1] - crow_ref[0, hd:hd + 1, :]
            if masked:
                s = _causal(s)
            m, l, acc = _softmax_step(s, v_ref[0, :, sl].astype(BF16), m_sc[hd], l_sc[hd], acc_sc[hd])
            m_sc[hd] = m
            l_sc[hd] = l
            acc_sc[hd] = acc

    @pl.when(j < nk)
    def _():
        step(ck_ref, cv_ref, cc_ref, False)

    @pl.when(j == nk)
    def _():
        step(kn_ref, vn_ref, cn_ref, True)
        for hd in range(N_HEADS):
            sl = slice(hd * HEAD_DIM, (hd + 1) * HEAD_DIM)
            o_ref[0, :, sl] = (acc_sc[hd] / l_sc[hd]).astype(o_ref.dtype)


def _sample_attention(q, cache_k, cache_v, k_new, v_new, cq_col, cc_row, cn_row):
    b, l, _ = q.shape
    past = cache_k.shape[1]
    tk = min(TK_CACHE, past)
    assert past % tk == 0
    nk = past // tk
    cache_spec = pl.BlockSpec((1, tk, ATTN_WIDTH), lambda bi, j: (bi, jnp.minimum(j, nk - 1), 0))
    new_spec = pl.BlockSpec((1, l, ATTN_WIDTH), lambda bi, j: (bi, 0, 0))
    return pl.pallas_call(
        functools.partial(_sample_attn_kernel, nk=nk),
        grid=(b, nk + 1),
        in_specs=[new_spec, cache_spec, cache_spec, new_spec, new_spec,
                  pl.BlockSpec((1, l, N_HEADS), lambda bi, j: (bi, 0, 0)),
                  pl.BlockSpec((1, N_HEADS, tk), lambda bi, j: (bi, 0, jnp.minimum(j, nk - 1))),
                  pl.BlockSpec((1, N_HEADS, l), lambda bi, j: (bi, 0, 0))],
        out_specs=new_spec,
        out_shape=jax.ShapeDtypeStruct((b, l, ATTN_WIDTH), BF16),
        scratch_shapes=[pltpu.VMEM((N_HEADS, l, 1), F32), pltpu.VMEM((N_HEADS, l, 1), F32),
                        pltpu.VMEM((N_HEADS, l, HEAD_DIM), F32)],
        compiler_params=_params(("parallel", "arbitrary")),
        name="sample_attention",
    )(q, cache_k, cache_v, k_new, v_new, cq_col, cc_row, cn_row)


def _finish_kernel(x_ref, po_ref, at_ref, wo_ref, gffn_ref, wr_ref, rb_ref, wsg_ref, wsu_ref, wsd_ref, cin_ref,
                   base_ref, hp_ref, idx_ref, rank_ref, gate_ref, cnt_ref, carry_sc):
    @pl.when(pl.program_id(0) == 0)
    def _():
        carry_sc[...] = cin_ref[...]

    tm = x_ref.shape[0]
    x1 = x_ref[...] + _dot(po_ref[...], wo_ref[:POOL_WIDTH, :]) + _dot(at_ref[...], wo_ref[POOL_WIDTH:, :])
    ms = jnp.mean(x1 * x1, axis=-1, keepdims=True)
    h = x1 * lax.rsqrt(ms + EPS) * gffn_ref[...]
    hb = h.astype(BF16)

    g = _dot(hb, wsg_ref[...])
    up = _dot(hb, wsu_ref[...])
    act = (g * _sigmoid(g)) * up
    base_ref[...] = x1 + _dot(act.astype(BF16), wsd_ref[...])

    bits = lax.bitcast_convert_type(hb.astype(F32), U32)
    hp_ref[...] = (bits[:, :HALF] >> 16) | (bits[:, HALF:] & jnp.uint32(0xFFFF0000))

    logits = jnp.dot(h, wr_ref[...], precision=lax.Precision.HIGHEST, preferred_element_type=F32)
    scores = _sigmoid(logits)
    sel = scores + rb_ref[...]
    lane = lax.broadcasted_iota(jnp.int32, (tm, N_EXPERTS), 1).astype(F32)
    mask = jnp.zeros((tm, N_EXPERTS), F32)
    idx_cols, gate_cols = [], []
    for _ in range(TOP_K):
        mx = jnp.max(sel, axis=1, keepdims=True)
        ik = jnp.min(jnp.where(sel == mx, lane, float(N_EXPERTS)), axis=1, keepdims=True)
        oh = lane == ik
        gate_cols.append(jnp.sum(jnp.where(oh, scores, 0.0), axis=1, keepdims=True))
        idx_cols.append(ik)
        sel = jnp.where(oh, -jnp.inf, sel)
        mask = jnp.where(oh, 1.0, mask)

    r = lax.broadcasted_iota(jnp.int32, (tm, tm), 0)
    c = lax.broadcasted_iota(jnp.int32, (tm, tm), 1)
    tri = (r > c).astype(BF16)
    rank = _dot(tri, mask.astype(BF16)) + carry_sc[...]
    carry_sc[...] = carry_sc[...] + jnp.sum(mask, axis=0, keepdims=True)
    cnt_ref[...] = carry_sc[...]

    gsum = gate_cols[0]
    for gk in gate_cols[1:]:
        gsum = gsum + gk
    lane_k = lax.broadcasted_iota(jnp.int32, (tm, TOP_K), 1)
    idx_out = jnp.zeros((tm, TOP_K), F32)
    rank_out = jnp.zeros((tm, TOP_K), F32)
    gate_out = jnp.zeros((tm, TOP_K), F32)
    for k in range(TOP_K):
        rk = jnp.sum(jnp.where(lane == idx_cols[k], rank, 0.0), axis=1, keepdims=True)
        idx_out = jnp.where(lane_k == k, idx_cols[k], idx_out)
        rank_out = jnp.where(lane_k == k, rk, rank_out)
        gate_out = jnp.where(lane_k == k, gate_cols[k] / gsum * ROUTED_SCALE, gate_out)
    idx_ref[...] = idx_out.astype(jnp.int32)
    rank_ref[...] = rank_out.astype(jnp.int32)
    gate_ref[...] = gate_out


def _finish(x, pool_out, attn, w_o, g_ffn, w_router, router_bias, ws_gate, ws_up, ws_down, counts_in):
    t = x.shape[0]
    tm = min(TM_PROJ, t)
    assert t % tm == 0
    row = lambda w: pl.BlockSpec((tm, w), lambda i: (i, 0))
    return pl.pallas_call(
        _finish_kernel,
        grid=(t // tm,),
        in_specs=[row(D_MODEL), row(POOL_WIDTH), row(ATTN_WIDTH), _resident(w_o.shape), _resident((1, D_MODEL)),
                  _resident(w_router.shape), _resident((1, N_EXPERTS)), _resident(ws_gate.shape),
                  _resident(ws_up.shape), _resident(ws_down.shape), _resident((1, N_EXPERTS))],
        out_specs=[row(D_MODEL), row(HALF), row(TOP_K), row(TOP_K), row(TOP_K),
                   pl.BlockSpec((1, N_EXPERTS), lambda i: (0, 0))],
        out_shape=[jax.ShapeDtypeStruct((t, D_MODEL), F32),
                   jax.ShapeDtypeStruct((t, HALF), U32),
                   jax.ShapeDtypeStruct((t, TOP_K), jnp.int32),
                   jax.ShapeDtypeStruct((t, TOP_K), jnp.int32),
                   jax.ShapeDtypeStruct((t, TOP_K), F32),
                   jax.ShapeDtypeStruct((1, N_EXPERTS), F32)],
        scratch_shapes=[pltpu.VMEM((1, N_EXPERTS), F32)],
        compiler_params=_params(("arbitrary",)),
        name="outproj_route",
    )(x, pool_out, attn, w_o, g_ffn, w_router, router_bias, ws_gate, ws_up, ws_down, counts_in)


def _dispatch_kernel(dest_ref, hp_ref, xs_in_ref, xs_ref, sem):
    del xs_in_ref
    tm = hp_ref.shape[0]

    def row_copy(r, d):
        return pltpu.make_async_copy(hp_ref.at[pl.ds(r, 1)], xs_ref.at[pl.ds(d, 1)], sem)

    def issue(r, carry):
        for k in range(TOP_K):
            row_copy(r, dest_ref[r * TOP_K + k]).start()
        return carry

    def drain(r, carry):
        for k in range(TOP_K):
            row_copy(r, dest_ref[r * TOP_K + k]).wait()
        return carry

    lax.fori_loop(0, tm, issue, 0)
    lax.fori_loop(0, tm, drain, 0)


def _dispatch(dest, hp, xs):
    t = hp.shape[0]
    tm = min(TM_ROWS, t)
    assert t % tm == 0
    return pl.pallas_call(
        _dispatch_kernel,
        grid=(t // tm,),
        in_specs=[pl.BlockSpec((tm * TOP_K,), lambda i: (i,), memory_space=pltpu.SMEM),
                  pl.BlockSpec((tm, HALF), lambda i: (i, 0)),
                  pl.BlockSpec(memory_space=pl.ANY)],
        out_specs=pl.BlockSpec(memory_space=pl.ANY),
        out_shape=jax.ShapeDtypeStruct(xs.shape, xs.dtype),
        scratch_shapes=[pltpu.SemaphoreType.DMA],
        input_output_aliases={2: 0},
        compiler_params=_params(("arbitrary",)),
        name="moe_dispatch",
    )(dest, hp, xs)


def _unpack(p):
    lo = lax.bitcast_convert_type(p << 16, F32).astype(BF16)
    hi = lax.bitcast_convert_type(p & jnp.uint32(0xFFFF0000), F32).astype(BF16)
    return lo, hi


def _expert_kernel(blk_e_ref, nblk_ref, xs_ref, wg_ref, wu_ref, wd_ref, y_ref, wg_sc, wu_sc, wd_sc):
    b = pl.program_id(0)
    e = blk_e_ref[b]
    e_prev = blk_e_ref[jnp.maximum(b - 1, 0)]

    @pl.when((b == 0) | (e != e_prev))
    def _():
        wg_sc[...] = wg_ref[0].astype(BF16)
        wu_sc[...] = wu_ref[0].astype(BF16)
        wd_sc[...] = wd_ref[0].astype(BF16)

    @pl.when(b < nblk_ref[0])
    def _():
        lo, hi = _unpack(xs_ref[...])
        g = _dot(lo, wg_sc[:HALF, :]) + _dot(hi, wg_sc[HALF:, :])
        up = _dot(lo, wu_sc[:HALF, :]) + _dot(hi, wu_sc[HALF:, :])
        act = (g * _sigmoid(g)) * up
        y_ref[...] = _dot(act.astype(BF16), wd_sc[...])


def _experts(blk_e, nblk, xs, w_gate, w_up, w_down):
    nb = blk_e.shape[0]
    m = M_EXPERT
    last = lambda b, nblk_ref: jnp.minimum(b, nblk_ref[0] - 1)
    grid_spec = pltpu.PrefetchScalarGridSpec(
        num_scalar_prefetch=2,
        grid=(nb,),
        in_specs=[pl.BlockSpec((m, HALF), lambda b, be, nr: (last(b, nr), 0)),
                  pl.BlockSpec((1, D_MODEL, D_EXPERT), lambda b, be, nr: (be[b], 0, 0)),
                  pl.BlockSpec((1, D_MODEL, D_EXPERT), lambda b, be, nr: (be[b], 0, 0)),
                  pl.BlockSpec((1, D_EXPERT, D_MODEL), lambda b, be, nr: (be[b], 0, 0))],
        out_specs=pl.BlockSpec((m, D_MODEL), lambda b, be, nr: (last(b, nr), 0)),
        scratch_shapes=[pltpu.VMEM((D_MODEL, D_EXPERT), BF16), pltpu.VMEM((D_MODEL, D_EXPERT), BF16),
                        pltpu.VMEM((D_EXPERT, D_MODEL), BF16)],
    )
    return pl.pallas_call(
        _expert_kernel,
        grid_spec=grid_spec,
        out_shape=jax.ShapeDtypeStruct((nb * m, D_MODEL), F32),
        compiler_params=_params(("arbitrary",)),
        name="moe_experts",
    )(blk_e, nblk, xs, w_gate, w_up, w_down)


def _combine_kernel(dest_ref, base_ref, gate_ref, ys_ref, o_ref, buf, sem):
    tm = base_ref.shape[0]

    def row_copy(r, k, d):
        return pltpu.make_async_copy(ys_ref.at[pl.ds(d, 1)], buf.at[k, pl.ds(r, 1)], sem)

    def issue(r, carry):
        for k in range(TOP_K):
            row_copy(r, k, dest_ref[r * TOP_K + k]).start()
        return carry

    def drain(r, carry):
        for k in range(TOP_K):
            row_copy(r, k, dest_ref[r * TOP_K + k]).wait()
        return carry

    lax.fori_loop(0, tm, issue, 0)
    lax.fori_loop(0, tm, drain, 0)
    acc = base_ref[...]
    for k in range(TOP_K):
        acc = acc + gate_ref[:, k:k + 1] * buf[k]
    o_ref[...] = acc


def _combine(dest, base, gates, ys):
    t = base.shape[0]
    tm = min(TM_ROWS, t)
    assert t % tm == 0
    return pl.pallas_call(
        _combine_kernel,
        grid=(t // tm,),
        in_specs=[pl.BlockSpec((tm * TOP_K,), lambda i: (i,), memory_space=pltpu.SMEM),
                  pl.BlockSpec((tm, D_MODEL), lambda i: (i, 0)),
                  pl.BlockSpec((tm, TOP_K), lambda i: (i, 0)),
                  pl.BlockSpec(memory_space=pl.ANY)],
        out_specs=pl.BlockSpec((tm, D_MODEL), lambda i: (i, 0)),
        out_shape=jax.ShapeDtypeStruct((t, D_MODEL), F32),
        scratch_shapes=[pltpu.VMEM((TOP_K, tm, D_MODEL), F32), pltpu.SemaphoreType.DMA],
        compiler_params=_params(("arbitrary",)),
        name="moe_combine",
    )(dest, base, gates, ys)


def _mixer(x, prefix, start_pos, w, init_c):
    b, l, _ = x.shape
    u, q, k, v, logf = _inproj(x.reshape(b * l, D_MODEL), w["g_mix"], w["w_main"], w["w_f"], w["b_f"],
                               w["g_q"], w["g_k"])
    u = u.reshape(b, l, POOL_WIDTH)
    pool_out = _pool(u, prefix, w["w_pool"], w["pool_scale"], start_pos)
    c = _cumsum(logf.reshape(b, l, N_HEADS), init_c)
    shp = (b, l, ATTN_WIDTH)
    return u, q.reshape(shp), k.reshape(shp), v.reshape(shp), logf, pool_out, c


def kernel(x_prompt, x_sample, cache_k, cache_v, cache_logf, cache_pool, g_mix, w_in, b_f, g_q, g_k, w_pool, pool_scale, w_o, g_ffn, w_router, router_bias, w_gate, w_up, w_down, ws_gate, ws_up, ws_down):
    depth = w_in.shape[0]
    assert depth == 1, "single-layer step"
    bp, sp, _ = x_prompt.shape
    bs, ls, _ = x_sample.shape
    past = cache_k.shape[2]
    assert sp >= POOL_BUF and ls >= POOL_BUF
    n_main = POOL_WIDTH + 3 * ATTN_WIDTH

    w = dict(
        g_mix=g_mix[0][None], g_q=g_q[0][None], g_k=g_k[0][None], b_f=b_f[0][None],
        w_main=w_in[0, :, :n_main].astype(BF16),
        w_f=jnp.pad(w_in[0, :, n_main:], ((0, 0), (0, HEAD_DIM - N_HEADS))).astype(BF16),
        w_pool=w_pool[0].astype(BF16), pool_scale=pool_scale[0][None],
    )
    fin = (w_o[0].astype(BF16), g_ffn[0][None], w_router[0], router_bias[0][None],
           ws_gate[0].astype(BF16), ws_up[0].astype(BF16), ws_down[0].astype(BF16))

    zero_c = lambda b: jnp.zeros((b, 1, N_HEADS), F32)

    u_p, q_p, k_p, v_p, logf_p, pool_p, c_p = _mixer(
        x_prompt, jnp.zeros((bp, POOL_HALO, POOL_WIDTH), F32), 0, w, zero_c(bp))
    c_row_p = c_p.transpose(0, 2, 1).reshape(bp * N_HEADS, 1, sp)
    attn_p = _prompt_attention(q_p, k_p, v_p, c_p, c_row_p)

    ck = cache_k[0].reshape(bs, past, ATTN_WIDTH)
    cv = cache_v[0].reshape(bs, past, ATTN_WIDTH)
    c_cache = _cumsum(cache_logf[0], zero_c(bs))
    prefix_s = jnp.pad(cache_pool[0], ((0, 0), (POOL_HALO - POOL_BUF, 0), (0, 0)))
    u_s, q_s, k_s, v_s, logf_s, pool_s, c_s = _mixer(x_sample, prefix_s, past, w, c_cache[:, past - 1:past, :])
    attn_s = _sample_attention(q_s, ck, cv, k_s, v_s, c_s, c_cache.transpose(0, 2, 1), c_s.transpose(0, 2, 1))

    tp, ts = bp * sp, bs * ls
    base_p, hp_p, idx_p, rank_p, gate_p, cnt_p = _finish(
        x_prompt.reshape(tp, D_MODEL), pool_p.reshape(tp, POOL_WIDTH), attn_p.reshape(tp, ATTN_WIDTH), *fin,
        jnp.zeros((1, N_EXPERTS), F32))
    base_s, hp_s, idx_s, rank_s, gate_s, cnt_s = _finish(
        x_sample.reshape(ts, D_MODEL), pool_s.reshape(ts, POOL_WIDTH), attn_s.reshape(ts, ATTN_WIDTH), *fin, cnt_p)

    m = M_EXPERT
    counts = cnt_s[0].astype(jnp.int32)
    pcounts = (counts + m - 1) // m * m
    pends = jnp.cumsum(pcounts)
    pstarts = pends - pcounts
    nb = -(-(tp + ts) * TOP_K // m) + N_EXPERTS
    blk_e = jnp.minimum(jnp.searchsorted(pends, jnp.arange(nb, dtype=jnp.int32) * m, side="right"),
                        N_EXPERTS - 1).astype(jnp.int32)
    nblk = (pends[-1:] // m).astype(jnp.int32)
    dest_p = (pstarts[idx_p] + rank_p).reshape(tp * TOP_K)
    dest_s = (pstarts[idx_s] + rank_s).reshape(ts * TOP_K)

    xs = jnp.zeros((nb * m, HALF), U32)
    xs = _dispatch(dest_p, hp_p, xs)
    xs = _dispatch(dest_s, hp_s, xs)
    ys = _experts(blk_e, nblk, xs, w_gate[0], w_up[0], w_down[0])
    y_p = _combine(dest_p, base_p, gate_p, ys).reshape(bp, sp, D_MODEL)
    y_s = _combine(dest_s, base_s, gate_s, ys).reshape(bs, ls, D_MODEL)

    heads = lambda a, b, l: a.reshape(1, b, l, N_HEADS, HEAD_DIM)
    return (y_p, y_s,
            heads(k_p, bp, sp), heads(v_p, bp, sp), logf_p.reshape(1, bp, sp, N_HEADS), u_p[None, :, sp - POOL_BUF:, :],
            heads(k_s, bs, ls), heads(v_s, bs, ls), logf_s.reshape(1, bs, ls, N_HEADS), u_s[None, :, ls - POOL_BUF:, :])
```

```python
import functools

import jax
import jax.numpy as jnp
from jax import lax
from jax.experimental import pallas as pl
from jax.experimental.pallas import tpu as pltpu

D_MODEL = 2048
POOL_WIDTH = 1024
ATTN_WIDTH = 1024
HEAD_DIM = 128
N_HEADS = 8
POOL_WINDOWS = (2, 4, 8, 16)
POOL_GROUP = 256
POOL_BUF = 15
POOL_HALO = 16
N_EXPERTS = 64
TOP_K = 8
D_EXPERT = 512
D_SHARED = 512
ROUTED_SCALE = 2.5
EPS = 1e-6
NEG = -1e30

F32 = jnp.float32
BF16 = jnp.bfloat16
U32 = jnp.uint32
HALF = D_MODEL // 2

V7X_VMEM_LIMIT = 56 * 1024 * 1024

TM_PROJ = 256
TL_SCAN = 512
TL_POOL = 512
TQ_ATTN = 512
TK_ATTN = 256
EXP_UNDERFLOW = 110.0
TK_CACHE = 1024
TM_ROWS = 128
M_EXPERT = 256


def _params(sem, vmem=V7X_VMEM_LIMIT):
    return pltpu.CompilerParams(dimension_semantics=sem, vmem_limit_bytes=vmem)


def _resident(shape):
    zeros = (0,) * len(shape)
    return pl.BlockSpec(shape, lambda *_: zeros, pipeline_mode=pl.Buffered(1))


def _dot(a, b):
    return jnp.dot(a, b, preferred_element_type=F32)


def _sigmoid(x):
    return 1.0 / (1.0 + jnp.exp(-x))


def _inproj_kernel(x_ref, gmix_ref, w_ref, wf_ref, bf_ref, gq_ref, gk_ref,
                   u_ref, q_ref, k_ref, v_ref, logf_ref, kb_ref, vb_ref):
    x = x_ref[...]
    ms = jnp.mean(x * x, axis=-1, keepdims=True)
    h = (x * lax.rsqrt(ms + EPS) * gmix_ref[...]).astype(BF16)
    p, a = POOL_WIDTH, ATTN_WIDTH
    u_ref[...] = _dot(h, w_ref[:, :p])
    zq = _dot(h, w_ref[:, p:p + a])
    zk = _dot(h, w_ref[:, p + a:p + 2 * a])
    v = _dot(h, w_ref[:, p + 2 * a:p + 3 * a])
    v_ref[...] = v
    vb_ref[...] = v.astype(BF16)
    scale = HEAD_DIM ** -0.5
    for hd in range(N_HEADS):
        sl = slice(hd * HEAD_DIM, (hd + 1) * HEAD_DIM)
        qh = zq[:, sl]
        qn = qh * lax.rsqrt(jnp.mean(qh * qh, axis=-1, keepdims=True) + EPS) * gq_ref[...]
        q_ref[:, sl] = (qn * scale).astype(BF16)
        kh = zk[:, sl]
        kn = kh * lax.rsqrt(jnp.mean(kh * kh, axis=-1, keepdims=True) + EPS) * gk_ref[...]
        k_ref[:, sl] = kn
        kb_ref[:, sl] = kn.astype(BF16)
    zf = _dot(h, wf_ref[...])[:, :N_HEADS] + bf_ref[...]
    logf_ref[...] = jnp.minimum(zf, 0.0) - jnp.log1p(jnp.exp(-jnp.abs(zf)))


def _inproj(x, g_mix, w_main, w_f, b_f, g_q, g_k):
    t = x.shape[0]
    tm = min(TM_PROJ, t)
    assert t % tm == 0
    row = lambda w: pl.BlockSpec((tm, w), lambda i: (i, 0))
    return pl.pallas_call(
        _inproj_kernel,
        grid=(t // tm,),
        in_specs=[row(D_MODEL), _resident((1, D_MODEL)), _resident(w_main.shape), _resident(w_f.shape),
                  _resident((1, N_HEADS)), _resident((1, HEAD_DIM)), _resident((1, HEAD_DIM))],
        out_specs=[row(POOL_WIDTH), row(ATTN_WIDTH), row(ATTN_WIDTH), row(ATTN_WIDTH), row(N_HEADS),
                   row(ATTN_WIDTH), row(ATTN_WIDTH)],
        out_shape=[jax.ShapeDtypeStruct((t, POOL_WIDTH), F32),
                   jax.ShapeDtypeStruct((t, ATTN_WIDTH), BF16),
                   jax.ShapeDtypeStruct((t, ATTN_WIDTH), F32),
                   jax.ShapeDtypeStruct((t, ATTN_WIDTH), F32),
                   jax.ShapeDtypeStruct((t, N_HEADS), F32),
                   jax.ShapeDtypeStruct((t, ATTN_WIDTH), BF16),
                   jax.ShapeDtypeStruct((t, ATTN_WIDTH), BF16)],
        compiler_params=_params(("arbitrary",)),
        name="inproj",
    )(x, g_mix, w_main, w_f, b_f, g_q, g_k)


def _cumsum_kernel(x_ref, init_ref, o_ref, carry_ref):
    @pl.when(pl.program_id(1) == 0)
    def _():
        carry_ref[...] = init_ref[0]

    x = x_ref[0]
    tl = x.shape[0]
    r = lax.broadcasted_iota(jnp.int32, (tl, tl), 0)
    c = lax.broadcasted_iota(jnp.int32, (tl, tl), 1)
    tri = (r >= c).astype(F32)
    cs = jnp.dot(tri, x, precision=lax.Precision.HIGHEST, preferred_element_type=F32) + carry_ref[...]
    o_ref[0] = cs
    carry_ref[...] = cs[tl - 1:tl, :]


def _cumsum(x, init):
    b, l, h = x.shape
    tl = min(TL_SCAN, l)
    assert l % tl == 0
    return pl.pallas_call(
        _cumsum_kernel,
        grid=(b, l // tl),
        in_specs=[pl.BlockSpec((1, tl, h), lambda bi, li: (bi, li, 0)),
                  pl.BlockSpec((1, 1, h), lambda bi, li: (bi, 0, 0))],
        out_specs=pl.BlockSpec((1, tl, h), lambda bi, li: (bi, li, 0)),
        out_shape=jax.ShapeDtypeStruct((b, l, h), F32),
        scratch_shapes=[pltpu.VMEM((1, h), F32)],
        compiler_params=_params(("arbitrary", "arbitrary")),
        name="logf_cumsum",
    )(x, init)


def _pool_kernel(u_ref, prefix_ref, wp_ref, ps_ref, o_ref, ext_ref, *, start_pos, tl):
    l = pl.program_id(1)

    @pl.when(l == 0)
    def _():
        ext_ref[0:POOL_HALO, :] = prefix_ref[0]

    u = u_ref[0]
    ext_ref[POOL_HALO:POOL_HALO + tl, :] = u
    pos = start_pos + l * tl + lax.broadcasted_iota(jnp.int32, (tl, 1), 0)
    for g, w in enumerate(POOL_WINDOWS):
        cols = slice(g * POOL_GROUP, (g + 1) * POOL_GROUP)
        s = u[:, cols]
        for j in range(1, w):
            s = s + ext_ref[POOL_HALO - j:POOL_HALO - j + tl, cols]
        cnt = jnp.minimum(w, pos + 1).astype(F32)
        d = s / cnt - u[:, cols]
        og = _dot(d.astype(BF16), wp_ref[g]) * ps_ref[:, cols]
        o_ref[0, :, cols] = og.astype(o_ref.dtype)
    ext_ref[0:POOL_HALO, :] = ext_ref[tl:tl + POOL_HALO, :]


def _pool(u, prefix, w_pool, pool_scale, start_pos):
    b, l, p = u.shape
    tl = min(TL_POOL, l)
    assert l % tl == 0 and tl >= POOL_HALO
    return pl.pallas_call(
        functools.partial(_pool_kernel, start_pos=start_pos, tl=tl),
        grid=(b, l // tl),
        in_specs=[pl.BlockSpec((1, tl, p), lambda bi, li: (bi, li, 0)),
                  pl.BlockSpec((1, POOL_HALO, p), lambda bi, li: (bi, 0, 0)),
                  _resident(w_pool.shape), _resident((1, p))],
        out_specs=pl.BlockSpec((1, tl, p), lambda bi, li: (bi, li, 0)),
        out_shape=jax.ShapeDtypeStruct((b, l, p), BF16),
        scratch_shapes=[pltpu.VMEM((tl + POOL_HALO, p), F32)],
        compiler_params=_params(("arbitrary", "arbitrary")),
        name="pool_mix",
    )(u, prefix, w_pool, pool_scale)


def _softmax_step(s, v, m_prev, l_prev, acc_prev):
    m_new = jnp.maximum(m_prev, jnp.max(s, axis=1, keepdims=True))
    alpha = jnp.exp(m_prev - m_new)
    p = jnp.exp(s - m_new)
    l_new = alpha * l_prev + jnp.sum(p, axis=1, keepdims=True)
    acc_new = alpha * acc_prev + _dot(p.astype(BF16), v)
    return m_new, l_new, acc_new


def _qk(q, k):
    return lax.dot_general(q, k, (((1,), (1,)), ((), ())), preferred_element_type=F32)


def _causal(s):
    r = lax.broadcasted_iota(jnp.int32, s.shape, 0)
    c = lax.broadcasted_iota(jnp.int32, s.shape, 1)
    return jnp.where(r >= c, s, NEG)


def _flash_kernel(tab_ref, q_ref, k_ref, v_ref, cq_ref, ck_ref, o_ref, m_sc, l_sc, acc_sc, cq_sc,
                  *, tq, tk, nq, nk, nbh):
    hd = pl.program_id(1)
    i = pl.program_id(2)
    bh = pl.program_id(0) * N_HEADS + hd
    r = tq // tk

    m_sc[...] = jnp.full(m_sc.shape, NEG, F32)
    l_sc[...] = jnp.zeros(l_sc.shape, F32)
    acc_sc[...] = jnp.zeros(acc_sc.shape, F32)
    cq = cq_ref[0]
    lane = lax.broadcasted_iota(jnp.int32, cq.shape, 1)
    cq_sc[...] = jnp.sum(jnp.where(lane == hd, cq, 0.0), axis=1, keepdims=True)
    q = q_ref[0]

    def tile(jj, masked):
        off = pl.multiple_of(jj * tk, tk)
        s = _qk(q, k_ref[0, pl.ds(off, tk), :]) + cq_sc[...] - ck_ref[0, pl.ds(jj, 1), :]
        if masked:
            rr = lax.broadcasted_iota(jnp.int32, s.shape, 0) + i * tq
            cc = lax.broadcasted_iota(jnp.int32, s.shape, 1) + off
            s = jnp.where(rr >= cc, s, NEG)
        m, l, acc = _softmax_step(s, v_ref[0, pl.ds(off, tk), :], m_sc[...], l_sc[...], acc_sc[...])
        m_sc[...] = m
        l_sc[...] = l
        acc_sc[...] = acc

    for d in range(r):
        tile(i * r + d, True)

    bound = tab_ref[0]
    cmax_i = tab_ref[1 + bh * nq + i]
    cmin_base = 1 + nbh * nq + bh * nk

    def live(jj):
        cmin_j = tab_ref[cmin_base + jnp.maximum(jj, 0)]
        return (jj >= 0) & (bound + cmax_i - cmin_j >= -EXP_UNDERFLOW)

    def body(jj):
        tile(jj, False)
        return jj - 1

    lax.while_loop(live, body, i * r - 1)
    o_ref[0] = (acc_sc[...] / l_sc[...]).astype(o_ref.dtype)


def _skip_table(c, tq, tk, qk_bound):
    b, s, h = c.shape
    cmax = c.reshape(b, s // tq, tq, h).max(axis=2)
    cmin = lax.cummin(c.reshape(b, s // tk, tk, h).min(axis=2), axis=1)
    flat = lambda a: a.transpose(0, 2, 1).reshape(-1)
    return jnp.concatenate([(2.0 * qk_bound + 2.0).reshape(1), flat(cmax), flat(cmin)]).astype(F32)


def _prompt_attention(q, k, v, c_col, qk_bound):
    b, s, _ = q.shape
    tq = min(TQ_ATTN, s)
    tk = min(TK_ATTN, tq)
    assert s % tq == 0 and tq % tk == 0
    nq, nk = s // tq, s // tk
    tab = _skip_table(c_col, tq, tk, qk_bound)
    c_row = c_col.transpose(0, 2, 1).reshape(b * N_HEADS, nk, tk)
    kv_spec = pl.BlockSpec((1, s, HEAD_DIM), lambda bi, hi, i, tab: (bi, 0, hi))
    grid_spec = pltpu.PrefetchScalarGridSpec(
        num_scalar_prefetch=1,
        grid=(b, N_HEADS, nq),
        in_specs=[pl.BlockSpec((1, tq, HEAD_DIM), lambda bi, hi, i, tab: (bi, i, hi)),
                  kv_spec, kv_spec,
                  pl.BlockSpec((1, tq, N_HEADS), lambda bi, hi, i, tab: (bi, i, 0)),
                  pl.BlockSpec((1, nk, tk), lambda bi, hi, i, tab: (bi * N_HEADS + hi, 0, 0))],
        out_specs=pl.BlockSpec((1, tq, HEAD_DIM), lambda bi, hi, i, tab: (bi, i, hi)),
        scratch_shapes=[pltpu.VMEM((tq, 1), F32), pltpu.VMEM((tq, 1), F32),
                        pltpu.VMEM((tq, HEAD_DIM), F32), pltpu.VMEM((tq, 1), F32)],
    )
    return pl.pallas_call(
        functools.partial(_flash_kernel, tq=tq, tk=tk, nq=nq, nk=nk, nbh=b * N_HEADS),
        grid_spec=grid_spec,
        out_shape=jax.ShapeDtypeStruct((b, s, ATTN_WIDTH), BF16),
        compiler_params=_params(("parallel", "parallel", "parallel")),
        name="prompt_attention",
    )(tab, q, k, v, c_col, c_row)


def _sample_attn_kernel(q_ref, ck_ref, cv_ref, kn_ref, vn_ref, cq_ref, cc_ref, cn_ref, o_ref,
                        m_sc, l_sc, acc_sc, *, nk, tk):
    j = pl.program_id(1)

    @pl.when(j == 0)
    def _():
        m_sc[...] = jnp.full(m_sc.shape, NEG, F32)
        l_sc[...] = jnp.zeros(l_sc.shape, F32)
        acc_sc[...] = jnp.zeros(acc_sc.shape, F32)

    def step(head_rows, k_ref, v_ref, crow_ref, masked):
        for hd in range(N_HEADS):
            sl = slice(hd * HEAD_DIM, (hd + 1) * HEAD_DIM)
            s = _qk(q_ref[0, :, sl], head_rows(k_ref, hd).astype(BF16))
            s = s + cq_ref[0, :, hd:hd + 1] - crow_ref[0, hd:hd + 1, :]
            if masked:
                s = _causal(s)
            m, l, acc = _softmax_step(s, head_rows(v_ref, hd).astype(BF16), m_sc[hd], l_sc[hd], acc_sc[hd])
            m_sc[hd] = m
            l_sc[hd] = l
            acc_sc[hd] = acc

    cached = lambda ref, hd: ref[0, pl.ds(hd, tk, stride=N_HEADS), :]
    fresh = lambda ref, hd: ref[0, :, hd * HEAD_DIM:(hd + 1) * HEAD_DIM]

    @pl.when(j < nk)
    def _():
        step(cached, ck_ref, cv_ref, cc_ref, False)

    @pl.when(j == nk)
    def _():
        step(fresh, kn_ref, vn_ref, cn_ref, True)
        for hd in range(N_HEADS):
            sl = slice(hd * HEAD_DIM, (hd + 1) * HEAD_DIM)
            o_ref[0, :, sl] = (acc_sc[hd] / l_sc[hd]).astype(o_ref.dtype)


def _sample_attention(q, cache_k, cache_v, k_new, v_new, cq_col, cc_row, cn_row):
    b, l, _ = q.shape
    past = cache_k.shape[1] // N_HEADS
    tk = min(TK_CACHE, past)
    assert past % tk == 0
    nk = past // tk
    cache_spec = pl.BlockSpec((1, tk * N_HEADS, HEAD_DIM), lambda bi, j: (bi, jnp.minimum(j, nk - 1), 0))
    new_spec = pl.BlockSpec((1, l, ATTN_WIDTH), lambda bi, j: (bi, 0, 0))
    return pl.pallas_call(
        functools.partial(_sample_attn_kernel, nk=nk, tk=tk),
        grid=(b, nk + 1),
        in_specs=[new_spec, cache_spec, cache_spec, new_spec, new_spec,
                  pl.BlockSpec((1, l, N_HEADS), lambda bi, j: (bi, 0, 0)),
                  pl.BlockSpec((1, N_HEADS, tk), lambda bi, j: (bi, 0, jnp.minimum(j, nk - 1))),
                  pl.BlockSpec((1, N_HEADS, l), lambda bi, j: (bi, 0, 0))],
        out_specs=new_spec,
        out_shape=jax.ShapeDtypeStruct((b, l, ATTN_WIDTH), BF16),
        scratch_shapes=[pltpu.VMEM((N_HEADS, l, 1), F32), pltpu.VMEM((N_HEADS, l, 1), F32),
                        pltpu.VMEM((N_HEADS, l, HEAD_DIM), F32)],
        compiler_params=_params(("parallel", "arbitrary")),
        name="sample_attention",
    )(q, cache_k, cache_v, k_new, v_new, cq_col, cc_row, cn_row)


def _finish_kernel(x_ref, po_ref, at_ref, wo_ref, gffn_ref, wrh_ref, wrl_ref, rb_ref, wsg_ref, wsu_ref, wsd_ref,
                   cin_ref,
                   base_ref, hp_ref, idx_ref, rank_ref, gate_ref, cnt_ref, carry_sc):
    @pl.when(pl.program_id(0) == 0)
    def _():
        carry_sc[...] = cin_ref[...]

    tm = x_ref.shape[0]
    x1 = x_ref[...] + _dot(po_ref[...], wo_ref[:POOL_WIDTH, :]) + _dot(at_ref[...], wo_ref[POOL_WIDTH:, :])
    ms = jnp.mean(x1 * x1, axis=-1, keepdims=True)
    h = x1 * lax.rsqrt(ms + EPS) * gffn_ref[...]
    hb = h.astype(BF16)

    g = _dot(hb, wsg_ref[...])
    up = _dot(hb, wsu_ref[...])
    act = (g * _sigmoid(g)) * up
    base_ref[...] = x1 + _dot(act.astype(BF16), wsd_ref[...])

    bits = lax.bitcast_convert_type(hb.astype(F32), U32)
    hp_ref[...] = (bits[:, :HALF] >> 16) | (bits[:, HALF:] & jnp.uint32(0xFFFF0000))

    h_lo = (h - hb.astype(F32)).astype(BF16)
    logits = _dot(hb, wrh_ref[...]) + (_dot(h_lo, wrh_ref[...]) + _dot(hb, wrl_ref[...]))
    scores = _sigmoid(logits)
    sel = scores + rb_ref[...]
    lane = lax.broadcasted_iota(jnp.int32, (tm, N_EXPERTS), 1).astype(F32)
    mask = jnp.zeros((tm, N_EXPERTS), F32)
    idx_cols, gate_cols = [], []
    for _ in range(TOP_K):
        mx = jnp.max(sel, axis=1, keepdims=True)
        ik = jnp.min(jnp.where(sel == mx, lane, float(N_EXPERTS)), axis=1, keepdims=True)
        oh = lane == ik
        gate_cols.append(jnp.sum(jnp.where(oh, scores, 0.0), axis=1, keepdims=True))
        idx_cols.append(ik)
        sel = jnp.where(oh, -jnp.inf, sel)
        mask = jnp.where(oh, 1.0, mask)

    r = lax.broadcasted_iota(jnp.int32, (tm, tm), 0)
    c = lax.broadcasted_iota(jnp.int32, (tm, tm), 1)
    tri = (r > c).astype(BF16)
    rank = _dot(tri, mask.astype(BF16)) + carry_sc[...]
    carry_sc[...] = carry_sc[...] + jnp.sum(mask, axis=0, keepdims=True)
    cnt_ref[...] = carry_sc[...]

    gsum = gate_cols[0]
    for gk in gate_cols[1:]:
        gsum = gsum + gk
    lane_k = lax.broadcasted_iota(jnp.int32, (tm, TOP_K), 1)
    idx_out = jnp.zeros((tm, TOP_K), F32)
    rank_out = jnp.zeros((tm, TOP_K), F32)
    gate_out = jnp.zeros((tm, TOP_K), F32)
    for k in range(TOP_K):
        rk = jnp.sum(jnp.where(lane == idx_cols[k], rank, 0.0), axis=1, keepdims=True)
        idx_out = jnp.where(lane_k == k, idx_cols[k], idx_out)
        rank_out = jnp.where(lane_k == k, rk, rank_out)
        gate_out = jnp.where(lane_k == k, gate_cols[k] / gsum * ROUTED_SCALE, gate_out)
    idx_ref[...] = idx_out.astype(jnp.int32)
    rank_ref[...] = rank_out.astype(jnp.int32)
    gate_ref[...] = gate_out


def _finish(x, pool_out, attn, w_o, g_ffn, wr_hi, wr_lo, router_bias, ws_gate, ws_up, ws_down, counts_in):
    t = x.shape[0]
    tm = min(TM_PROJ, t)
    assert t % tm == 0
    row = lambda w: pl.BlockSpec((tm, w), lambda i: (i, 0))
    return pl.pallas_call(
        _finish_kernel,
        grid=(t // tm,),
        in_specs=[row(D_MODEL), row(POOL_WIDTH), row(ATTN_WIDTH), _resident(w_o.shape), _resident((1, D_MODEL)),
                  _resident(wr_hi.shape), _resident(wr_lo.shape), _resident((1, N_EXPERTS)),
                  _resident(ws_gate.shape), _resident(ws_up.shape), _resident(ws_down.shape),
                  _resident((1, N_EXPERTS))],
        out_specs=[row(D_MODEL), row(HALF), row(TOP_K), row(TOP_K), row(TOP_K),
                   pl.BlockSpec((1, N_EXPERTS), lambda i: (0, 0))],
        out_shape=[jax.ShapeDtypeStruct((t, D_MODEL), F32),
                   jax.ShapeDtypeStruct((t, HALF), U32),
                   jax.ShapeDtypeStruct((t, TOP_K), jnp.int32),
                   jax.ShapeDtypeStruct((t, TOP_K), jnp.int32),
                   jax.ShapeDtypeStruct((t, TOP_K), F32),
                   jax.ShapeDtypeStruct((1, N_EXPERTS), F32)],
        scratch_shapes=[pltpu.VMEM((1, N_EXPERTS), F32)],
        compiler_params=_params(("arbitrary",)),
        name="outproj_route",
    )(x, pool_out, attn, w_o, g_ffn, wr_hi, wr_lo, router_bias, ws_gate, ws_up, ws_down, counts_in)


def _dispatch_kernel(dest_ref, hp_ref, *rest):
    xs_ref, sem = rest[-2:]
    tm = hp_ref.shape[0]

    def row_copy(r, d):
        return pltpu.make_async_copy(hp_ref.at[pl.ds(r, 1)], xs_ref.at[pl.ds(d, 1)], sem)

    def issue(r, carry):
        for k in range(TOP_K):
            row_copy(r, dest_ref[r * TOP_K + k]).start(priority=k % 2)
        return carry

    def drain(r, carry):
        for k in range(TOP_K):
            row_copy(r, dest_ref[r * TOP_K + k]).wait()
        return carry

    lax.fori_loop(0, tm, issue, 0)
    lax.fori_loop(0, tm, drain, 0)


def _dispatch(dest, hp, xs=None, n_rows=None):
    t = hp.shape[0]
    tm = min(TM_ROWS, t)
    assert t % tm == 0
    in_specs = [pl.BlockSpec((tm * TOP_K,), lambda i: (i,), memory_space=pltpu.SMEM),
                pl.BlockSpec((tm, HALF), lambda i: (i, 0))]
    args = [dest, hp]
    if xs is not None:
        in_specs.append(pl.BlockSpec(memory_space=pl.ANY))
        args.append(xs)
        n_rows = xs.shape[0]
    return pl.pallas_call(
        _dispatch_kernel,
        grid=(t // tm,),
        in_specs=in_specs,
        out_specs=pl.BlockSpec(memory_space=pl.ANY),
        out_shape=jax.ShapeDtypeStruct((n_rows, HALF), U32),
        scratch_shapes=[pltpu.SemaphoreType.DMA],
        input_output_aliases={2: 0} if xs is not None else {},
        compiler_params=_params(("arbitrary",)),
        name="moe_dispatch",
    )(*args)


def _unpack(p):
    lo = lax.bitcast_convert_type(p << 16, F32).astype(BF16)
    hi = lax.bitcast_convert_type(p & jnp.uint32(0xFFFF0000), F32).astype(BF16)
    return lo, hi


def _expert_kernel(blk_e_ref, nblk_ref, valid_ref, xs_ref, wg_ref, wu_ref, wd_ref, y_ref, wg_sc, wu_sc, wd_sc):
    b = pl.program_id(0)
    e = blk_e_ref[b]
    e_prev = blk_e_ref[jnp.maximum(b - 1, 0)]

    @pl.when((b == 0) | (e != e_prev))
    def _():
        wg_sc[...] = wg_ref[0].astype(BF16)
        wu_sc[...] = wu_ref[0].astype(BF16)
        wd_sc[...] = wd_ref[0].astype(BF16)

    @pl.when(b < nblk_ref[0])
    def _():
        row = lax.broadcasted_iota(jnp.int32, (xs_ref.shape[0], 1), 0)
        lo, hi = _unpack(jnp.where(row < valid_ref[b], xs_ref[...], jnp.uint32(0)))
        g = _dot(lo, wg_sc[:HALF, :]) + _dot(hi, wg_sc[HALF:, :])
        up = _dot(lo, wu_sc[:HALF, :]) + _dot(hi, wu_sc[HALF:, :])
        act = (g * _sigmoid(g)) * up
        y_ref[...] = _dot(act.astype(BF16), wd_sc[...])


def _experts(blk_e, nblk, blk_valid, xs, w_gate, w_up, w_down):
    nb = blk_e.shape[0]
    m = M_EXPERT
    last = lambda b, nblk_ref: jnp.minimum(b, nblk_ref[0] - 1)
    grid_spec = pltpu.PrefetchScalarGridSpec(
        num_scalar_prefetch=3,
        grid=(nb,),
        in_specs=[pl.BlockSpec((m, HALF), lambda b, be, nr, va: (last(b, nr), 0)),
                  pl.BlockSpec((1, D_MODEL, D_EXPERT), lambda b, be, nr, va: (be[b], 0, 0)),
                  pl.BlockSpec((1, D_MODEL, D_EXPERT), lambda b, be, nr, va: (be[b], 0, 0)),
                  pl.BlockSpec((1, D_EXPERT, D_MODEL), lambda b, be, nr, va: (be[b], 0, 0))],
        out_specs=pl.BlockSpec((m, D_MODEL), lambda b, be, nr, va: (last(b, nr), 0)),
        scratch_shapes=[pltpu.VMEM((D_MODEL, D_EXPERT), BF16), pltpu.VMEM((D_MODEL, D_EXPERT), BF16),
                        pltpu.VMEM((D_EXPERT, D_MODEL), BF16)],
    )
    return pl.pallas_call(
        _expert_kernel,
        grid_spec=grid_spec,
        out_shape=jax.ShapeDtypeStruct((nb * m, D_MODEL), F32),
        compiler_params=_params(("arbitrary",)),
        name="moe_experts",
    )(blk_e, nblk, blk_valid, xs, w_gate, w_up, w_down)


def _combine_kernel(dest_ref, base_ref, gate_ref, ys_ref, o_ref, buf, sem):
    tm = base_ref.shape[0]

    def row_copy(r, k, d):
        return pltpu.make_async_copy(ys_ref.at[pl.ds(d, 1)], buf.at[k, pl.ds(r, 1)], sem)

    def issue(r, carry):
        for k in range(TOP_K):
            row_copy(r, k, dest_ref[r * TOP_K + k]).start(priority=k % 2)
        return carry

    def drain(r, carry):
        for k in range(TOP_K):
            row_copy(r, k, dest_ref[r * TOP_K + k]).wait()
        return carry

    lax.fori_loop(0, tm, issue, 0)
    lax.fori_loop(0, tm, drain, 0)
    acc = base_ref[...]
    for k in range(TOP_K):
        acc = acc + gate_ref[:, k:k + 1] * buf[k]
    o_ref[...] = acc


def _combine(dest, base, gates, ys):
    t = base.shape[0]
    tm = min(TM_ROWS, t)
    assert t % tm == 0
    return pl.pallas_call(
        _combine_kernel,
        grid=(t // tm,),
        in_specs=[pl.BlockSpec((tm * TOP_K,), lambda i: (i,), memory_space=pltpu.SMEM),
                  pl.BlockSpec((tm, D_MODEL), lambda i: (i, 0)),
                  pl.BlockSpec((tm, TOP_K), lambda i: (i, 0)),
                  pl.BlockSpec(memory_space=pl.ANY)],
        out_specs=pl.BlockSpec((tm, D_MODEL), lambda i: (i, 0)),
        out_shape=jax.ShapeDtypeStruct((t, D_MODEL), F32),
        scratch_shapes=[pltpu.VMEM((TOP_K, tm, D_MODEL), F32), pltpu.SemaphoreType.DMA],
        compiler_params=_params(("arbitrary",)),
        name="moe_combine",
    )(dest, base, gates, ys)


def _mixer(x, prefix, start_pos, w, init_c):
    b, l, _ = x.shape
    u, q, k, v, logf, kb, vb = _inproj(x.reshape(b * l, D_MODEL), w["g_mix"], w["w_main"], w["w_f"], w["b_f"],
                                       w["g_q"], w["g_k"])
    u = u.reshape(b, l, POOL_WIDTH)
    pool_out = _pool(u, prefix, w["w_pool"], w["pool_scale"], start_pos)
    c = _cumsum(logf.reshape(b, l, N_HEADS), init_c)
    shp = (b, l, ATTN_WIDTH)
    return u, q.reshape(shp), k.reshape(shp), v.reshape(shp), logf, pool_out, c, kb.reshape(shp), vb.reshape(shp)


def _slot_starts(idx, pstarts):
    experts = jnp.arange(N_EXPERTS, dtype=jnp.int32)
    return jnp.sum(jnp.where(idx[..., None] == experts, pstarts, 0), axis=-1)


def kernel(x_prompt, x_sample, cache_k, cache_v, cache_logf, cache_pool, g_mix, w_in, b_f, g_q, g_k, w_pool, pool_scale, w_o, g_ffn, w_router, router_bias, w_gate, w_up, w_down, ws_gate, ws_up, ws_down):
    depth = w_in.shape[0]
    assert depth == 1, "single-layer step"
    bp, sp, _ = x_prompt.shape
    bs, ls, _ = x_sample.shape
    past = cache_k.shape[2]
    assert sp >= POOL_BUF and ls >= POOL_BUF
    n_main = POOL_WIDTH + 3 * ATTN_WIDTH

    w = dict(
        g_mix=g_mix[0][None], g_q=g_q[0][None], g_k=g_k[0][None], b_f=b_f[0][None],
        w_main=w_in[0, :, :n_main].astype(BF16),
        w_f=jnp.pad(w_in[0, :, n_main:], ((0, 0), (0, HEAD_DIM - N_HEADS))).astype(BF16),
        w_pool=w_pool[0].astype(BF16), pool_scale=pool_scale[0][None],
    )
    wr_hi = w_router[0].astype(BF16)
    wr_lo = (w_router[0] - wr_hi.astype(F32)).astype(BF16)
    fin = (w_o[0].astype(BF16), g_ffn[0][None], wr_hi, wr_lo, router_bias[0][None],
           ws_gate[0].astype(BF16), ws_up[0].astype(BF16), ws_down[0].astype(BF16))
    qk_bound = 1.02 * HEAD_DIM ** 0.5 * jnp.max(jnp.abs(g_q[0])) * jnp.max(jnp.abs(g_k[0]))

    zero_c = lambda b: jnp.zeros((b, 1, N_HEADS), F32)

    u_p, q_p, k_p, v_p, logf_p, pool_p, c_p, kb_p, vb_p = _mixer(
        x_prompt, jnp.zeros((bp, POOL_HALO, POOL_WIDTH), F32), 0, w, zero_c(bp))
    attn_p = _prompt_attention(q_p, kb_p, vb_p, c_p, qk_bound)

    ck = cache_k[0].reshape(bs, past * N_HEADS, HEAD_DIM)
    cv = cache_v[0].reshape(bs, past * N_HEADS, HEAD_DIM)
    c_cache = _cumsum(cache_logf[0], zero_c(bs))
    prefix_s = jnp.pad(cache_pool[0], ((0, 0), (POOL_HALO - POOL_BUF, 0), (0, 0)))
    u_s, q_s, k_s, v_s, logf_s, pool_s, c_s, _, _ = _mixer(x_sample, prefix_s, past, w,
                                                           c_cache[:, past - 1:past, :])
    attn_s = _sample_attention(q_s, ck, cv, k_s, v_s, c_s, c_cache.transpose(0, 2, 1), c_s.transpose(0, 2, 1))

    tp, ts = bp * sp, bs * ls
    base_p, hp_p, idx_p, rank_p, gate_p, cnt_p = _finish(
        x_prompt.reshape(tp, D_MODEL), pool_p.reshape(tp, POOL_WIDTH), attn_p.reshape(tp, ATTN_WIDTH), *fin,
        jnp.zeros((1, N_EXPERTS), F32))
    base_s, hp_s, idx_s, rank_s, gate_s, cnt_s = _finish(
        x_sample.reshape(ts, D_MODEL), pool_s.reshape(ts, POOL_WIDTH), attn_s.reshape(ts, ATTN_WIDTH), *fin, cnt_p)

    m = M_EXPERT
    counts = cnt_s[0].astype(jnp.int32)
    pcounts = (counts + m - 1) // m * m
    pends = jnp.cumsum(pcounts)
    pstarts = pends - pcounts
    nb = -(-(tp + ts) * TOP_K // m) + N_EXPERTS
    blk_row = jnp.arange(nb, dtype=jnp.int32)[:, None] * m
    blk_e = jnp.minimum(jnp.sum((pends[None, :] <= blk_row).astype(jnp.int32), axis=1), N_EXPERTS - 1)
    nblk = (pends[-1:] // m).astype(jnp.int32)
    of_blk = blk_e[:, None] == jnp.arange(N_EXPERTS, dtype=jnp.int32)
    blk_off = blk_row[:, 0] - jnp.sum(jnp.where(of_blk, pstarts, 0), axis=1)
    blk_valid = jnp.clip(jnp.sum(jnp.where(of_blk, counts, 0), axis=1) - blk_off, 0, m).astype(jnp.int32)
    dest_p = (_slot_starts(idx_p, pstarts) + rank_p).reshape(tp * TOP_K)
    dest_s = (_slot_starts(idx_s, pstarts) + rank_s).reshape(ts * TOP_K)

    xs = _dispatch(dest_p, hp_p, n_rows=nb * m)
    xs = _dispatch(dest_s, hp_s, xs)
    ys = _experts(blk_e, nblk, blk_valid, xs, w_gate[0], w_up[0], w_down[0])
    y_p = _combine(dest_p, base_p, gate_p, ys).reshape(bp, sp, D_MODEL)
    y_s = _combine(dest_s, base_s, gate_s, ys).reshape(bs, ls, D_MODEL)

    heads = lambda a, b, l: a.reshape(1, b, l, N_HEADS, HEAD_DIM)
    return (y_p, y_s,
            heads(k_p, bp, sp), heads(v_p, bp, sp), logf_p.reshape(1, bp, sp, N_HEADS), u_p[None, :, sp - POOL_BUF:, :],
            heads(k_s, bs, ls), heads(v_s, bs, ls), logf_s.reshape(1, bs, ls, N_HEADS), u_s[None, :, ls - POOL_BUF:, :])
```

```python
import functools

import jax
import jax.numpy as jnp
from jax import lax
from jax.experimental import pallas as pl
from jax.experimental.pallas import tpu as pltpu

D_MODEL = 2048
POOL_WIDTH = 1024
ATTN_WIDTH = 1024
HEAD_DIM = 128
N_HEADS = 8
POOL_WINDOWS = (2, 4, 8, 16)
POOL_GROUP = 256
POOL_BUF = 15
POOL_HALO = 16
N_EXPERTS = 64
TOP_K = 8
D_EXPERT = 512
D_SHARED = 512
ROUTED_SCALE = 2.5
EPS = 1e-6
NEG = -1e30

F32 = jnp.float32
BF16 = jnp.bfloat16
U32 = jnp.uint32
HALF = D_MODEL // 2

V7X_VMEM_LIMIT = 56 * 1024 * 1024

TM_PROJ = 256
TL_SCAN = 512
TL_POOL = 512
TQ_ATTN = 512
TK_ATTN = 256
EXP_UNDERFLOW = 110.0
STATIC_SOFTMAX_RANGE = 60.0
TK_CACHE = 1024
TM_ROWS = 128
M_EXPERT = 256
COMBINE_ROWS = 8


def _params(sem, vmem=V7X_VMEM_LIMIT):
    return pltpu.CompilerParams(dimension_semantics=sem, vmem_limit_bytes=vmem)


def _resident(shape):
    zeros = (0,) * len(shape)
    return pl.BlockSpec(shape, lambda *_: zeros, pipeline_mode=pl.Buffered(1))


def _dot(a, b):
    return jnp.dot(a, b, preferred_element_type=F32)


def _sigmoid(x):
    return 1.0 / (1.0 + jnp.exp(-x))


def _inproj_kernel(x_ref, gmix_ref, w_ref, wf_ref, bf_ref, gq_ref, gk_ref,
                   u_ref, q_ref, k_ref, v_ref, logf_ref, kb_ref, vb_ref):
    x = x_ref[...]
    ms = jnp.mean(x * x, axis=-1, keepdims=True)
    h = (x * lax.rsqrt(ms + EPS) * gmix_ref[...]).astype(BF16)
    p, a = POOL_WIDTH, ATTN_WIDTH
    u_ref[...] = _dot(h, w_ref[:, :p])
    zq = _dot(h, w_ref[:, p:p + a])
    zk = _dot(h, w_ref[:, p + a:p + 2 * a])
    v = _dot(h, w_ref[:, p + 2 * a:p + 3 * a])
    v_ref[...] = v
    vb_ref[...] = v.astype(BF16)
    scale = HEAD_DIM ** -0.5
    for hd in range(N_HEADS):
        sl = slice(hd * HEAD_DIM, (hd + 1) * HEAD_DIM)
        qh = zq[:, sl]
        qn = qh * lax.rsqrt(jnp.mean(qh * qh, axis=-1, keepdims=True) + EPS) * gq_ref[...]
        q_ref[:, sl] = (qn * scale).astype(BF16)
        kh = zk[:, sl]
        kn = kh * lax.rsqrt(jnp.mean(kh * kh, axis=-1, keepdims=True) + EPS) * gk_ref[...]
        k_ref[:, sl] = kn
        kb_ref[:, sl] = kn.astype(BF16)
    zf = _dot(h, wf_ref[...])[:, :N_HEADS] + bf_ref[...]
    logf_ref[...] = jnp.minimum(zf, 0.0) - jnp.log1p(jnp.exp(-jnp.abs(zf)))


def _inproj(x, g_mix, w_main, w_f, b_f, g_q, g_k):
    t = x.shape[0]
    tm = min(TM_PROJ, t)
    assert t % tm == 0
    row = lambda w: pl.BlockSpec((tm, w), lambda i: (i, 0))
    return pl.pallas_call(
        _inproj_kernel,
        grid=(t // tm,),
        in_specs=[row(D_MODEL), _resident((1, D_MODEL)), _resident(w_main.shape), _resident(w_f.shape),
                  _resident((1, N_HEADS)), _resident((1, HEAD_DIM)), _resident((1, HEAD_DIM))],
        out_specs=[row(POOL_WIDTH), row(ATTN_WIDTH), row(ATTN_WIDTH), row(ATTN_WIDTH), row(N_HEADS),
                   row(ATTN_WIDTH), row(ATTN_WIDTH)],
        out_shape=[jax.ShapeDtypeStruct((t, POOL_WIDTH), F32),
                   jax.ShapeDtypeStruct((t, ATTN_WIDTH), BF16),
                   jax.ShapeDtypeStruct((t, ATTN_WIDTH), F32),
                   jax.ShapeDtypeStruct((t, ATTN_WIDTH), F32),
                   jax.ShapeDtypeStruct((t, N_HEADS), F32),
                   jax.ShapeDtypeStruct((t, ATTN_WIDTH), BF16),
                   jax.ShapeDtypeStruct((t, ATTN_WIDTH), BF16)],
        compiler_params=_params(("arbitrary",)),
        name="inproj",
    )(x, g_mix, w_main, w_f, b_f, g_q, g_k)


def _cumsum_kernel(x_ref, init_ref, o_ref, carry_ref):
    @pl.when(pl.program_id(1) == 0)
    def _():
        carry_ref[...] = init_ref[0]

    x = x_ref[0]
    tl = x.shape[0]
    r = lax.broadcasted_iota(jnp.int32, (tl, tl), 0)
    c = lax.broadcasted_iota(jnp.int32, (tl, tl), 1)
    tri = (r >= c).astype(F32)
    cs = jnp.dot(tri, x, precision=lax.Precision.HIGHEST, preferred_element_type=F32) + carry_ref[...]
    o_ref[0] = cs
    carry_ref[...] = cs[tl - 1:tl, :]


def _cumsum(x, init):
    b, l, h = x.shape
    tl = min(TL_SCAN, l)
    assert l % tl == 0
    return pl.pallas_call(
        _cumsum_kernel,
        grid=(b, l // tl),
        in_specs=[pl.BlockSpec((1, tl, h), lambda bi, li: (bi, li, 0)),
                  pl.BlockSpec((1, 1, h), lambda bi, li: (bi, 0, 0))],
        out_specs=pl.BlockSpec((1, tl, h), lambda bi, li: (bi, li, 0)),
        out_shape=jax.ShapeDtypeStruct((b, l, h), F32),
        scratch_shapes=[pltpu.VMEM((1, h), F32)],
        compiler_params=_params(("arbitrary", "arbitrary")),
        name="logf_cumsum",
    )(x, init)


def _pool_kernel(u_ref, prefix_ref, wp_ref, ps_ref, o_ref, ext_ref, *, start_pos, tl):
    l = pl.program_id(1)

    @pl.when(l == 0)
    def _():
        ext_ref[0:POOL_HALO, :] = prefix_ref[0]

    u = u_ref[0]
    ext_ref[POOL_HALO:POOL_HALO + tl, :] = u
    pos = start_pos + l * tl + lax.broadcasted_iota(jnp.int32, (tl, 1), 0)
    for g, w in enumerate(POOL_WINDOWS):
        cols = slice(g * POOL_GROUP, (g + 1) * POOL_GROUP)
        s = u[:, cols]
        for j in range(1, w):
            s = s + ext_ref[POOL_HALO - j:POOL_HALO - j + tl, cols]
        cnt = jnp.minimum(w, pos + 1).astype(F32)
        d = s / cnt - u[:, cols]
        og = _dot(d.astype(BF16), wp_ref[g]) * ps_ref[:, cols]
        o_ref[0, :, cols] = og.astype(o_ref.dtype)
    ext_ref[0:POOL_HALO, :] = ext_ref[tl:tl + POOL_HALO, :]


def _pool(u, prefix, w_pool, pool_scale, start_pos):
    b, l, p = u.shape
    tl = min(TL_POOL, l)
    assert l % tl == 0 and tl >= POOL_HALO
    return pl.pallas_call(
        functools.partial(_pool_kernel, start_pos=start_pos, tl=tl),
        grid=(b, l // tl),
        in_specs=[pl.BlockSpec((1, tl, p), lambda bi, li: (bi, li, 0)),
                  pl.BlockSpec((1, POOL_HALO, p), lambda bi, li: (bi, 0, 0)),
                  _resident(w_pool.shape), _resident((1, p))],
        out_specs=pl.BlockSpec((1, tl, p), lambda bi, li: (bi, li, 0)),
        out_shape=jax.ShapeDtypeStruct((b, l, p), BF16),
        scratch_shapes=[pltpu.VMEM((tl + POOL_HALO, p), F32)],
        compiler_params=_params(("arbitrary", "arbitrary")),
        name="pool_mix",
    )(u, prefix, w_pool, pool_scale)


def _softmax_step(s, v, m_prev, l_prev, acc_prev):
    m_new = jnp.maximum(m_prev, jnp.max(s, axis=1, keepdims=True))
    alpha = jnp.exp(m_prev - m_new)
    p = jnp.exp(s - m_new)
    l_new = alpha * l_prev + jnp.sum(p, axis=1, keepdims=True)
    acc_new = alpha * acc_prev + _dot(p.astype(BF16), v)
    return m_new, l_new, acc_new


def _qk(q, k):
    return lax.dot_general(q, k, (((1,), (1,)), ((), ())), preferred_element_type=F32)


def _causal(s):
    r = lax.broadcasted_iota(jnp.int32, s.shape, 0)
    c = lax.broadcasted_iota(jnp.int32, s.shape, 1)
    return jnp.where(r >= c, s, NEG)


def _flash_kernel(tab_ref, q_ref, k_ref, v_ref, cq_ref, ck_ref, o_ref, m_sc, l_sc, acc_sc, cq_sc,
                  *, tq, tk, nq, nk, nbh):
    hd = pl.program_id(1)
    i = pl.program_id(2)
    bh = pl.program_id(0) * N_HEADS + hd
    r = tq // tk

    m_sc[...] = jnp.full(m_sc.shape, NEG, F32)
    l_sc[...] = jnp.zeros(l_sc.shape, F32)
    acc_sc[...] = jnp.zeros(acc_sc.shape, F32)
    cq = cq_ref[0]
    lane = lax.broadcasted_iota(jnp.int32, cq.shape, 1)
    cq_sc[...] = jnp.sum(jnp.where(lane == hd, cq, 0.0), axis=1, keepdims=True)
    q = q_ref[0]

    def tile(jj, masked):
        off = pl.multiple_of(jj * tk, tk)
        s = _qk(q, k_ref[0, pl.ds(off, tk), :]) + cq_sc[...] - ck_ref[0, pl.ds(jj, 1), :]
        if masked:
            rr = lax.broadcasted_iota(jnp.int32, s.shape, 0) + i * tq
            cc = lax.broadcasted_iota(jnp.int32, s.shape, 1) + off
            s = jnp.where(rr >= cc, s, NEG)
        m, l, acc = _softmax_step(s, v_ref[0, pl.ds(off, tk), :], m_sc[...], l_sc[...], acc_sc[...])
        m_sc[...] = m
        l_sc[...] = l
        acc_sc[...] = acc

    for d in range(r):
        tile(i * r + d, True)

    bound = tab_ref[0]
    cmax_i = tab_ref[2 + bh * nq + i]
    cmin_base = 2 + nbh * nq + bh * nk

    def live(jj):
        cmin_j = tab_ref[cmin_base + jnp.maximum(jj, 0)]
        return (jj >= 0) & (bound + cmax_i - cmin_j >= -EXP_UNDERFLOW)

    def body(jj):
        tile(jj, False)
        return jj - 1

    lax.while_loop(live, body, i * r - 1)
    o_ref[0] = (acc_sc[...] / l_sc[...]).astype(o_ref.dtype)


def _flash_static_kernel(tab_ref, q_ref, k_ref, v_ref, cq_ref, ck_ref, o_ref, vaug_sc, acc_sc, cqb_sc,
                         *, tq, tk, nq, nk, nbh):
    hd = pl.program_id(1)
    i = pl.program_id(2)
    bh = pl.program_id(0) * N_HEADS + hd
    r = tq // tk

    @pl.when(i == 0)
    def _():
        vaug_sc[:, :HEAD_DIM] = v_ref[0]
        vaug_sc[:, HEAD_DIM:] = jnp.ones((vaug_sc.shape[0], HEAD_DIM), BF16)

    cq = cq_ref[0]
    lane = lax.broadcasted_iota(jnp.int32, cq.shape, 1)
    col = jnp.sum(jnp.where(lane == hd, cq, 0.0), axis=1, keepdims=True) - tab_ref[1]
    cqb_sc[...] = jnp.broadcast_to(col, cqb_sc.shape)
    acc_sc[...] = jnp.zeros(acc_sc.shape, F32)
    q = q_ref[0]

    def tile(jj, masked):
        off = pl.multiple_of(jj * tk, tk)
        s = _qk(q, k_ref[0, pl.ds(off, tk), :])
        ck = ck_ref[0, pl.ds(jj, 1), :]
        parts = []
        for c in range(tk // HEAD_DIM):
            sl = slice(c * HEAD_DIM, (c + 1) * HEAD_DIM)
            sc = s[:, sl] + cqb_sc[...] - ck[:, sl]
            if masked:
                rr = lax.broadcasted_iota(jnp.int32, sc.shape, 0) + i * tq
                cc = lax.broadcasted_iota(jnp.int32, sc.shape, 1) + (off + c * HEAD_DIM)
                sc = jnp.where(rr >= cc, sc, NEG)
            parts.append(jnp.exp(sc).astype(BF16))
        acc_sc[...] += _dot(jnp.concatenate(parts, axis=1), vaug_sc[pl.ds(off, tk), :])

    for d in range(r):
        tile(i * r + d, True)

    cmax_i = tab_ref[2 + bh * nq + i]
    cmin_base = 2 + nbh * nq + bh * nk

    def live(jj):
        cmin_j = tab_ref[cmin_base + jnp.maximum(jj, 0)]
        return (jj >= 0) & (tab_ref[0] + cmax_i - cmin_j >= -EXP_UNDERFLOW)

    def body(jj):
        tile(jj, False)
        return jj - 1

    lax.while_loop(live, body, i * r - 1)
    o_ref[0] = (acc_sc[:, :HEAD_DIM] / acc_sc[:, HEAD_DIM:]).astype(o_ref.dtype)


def _skip_table(c, tq, tk, slack, shift):
    b, s, h = c.shape
    cmax = c.reshape(b, s // tq, tq, h).max(axis=2)
    cmin = lax.cummin(c.reshape(b, s // tk, tk, h).min(axis=2), axis=1)
    flat = lambda a: a.transpose(0, 2, 1).reshape(-1)
    return jnp.concatenate([slack.reshape(1), shift.reshape(1), flat(cmax), flat(cmin)]).astype(F32)


def _prompt_attention(q, k, v, c_col, qk_bound):
    b, s, _ = q.shape
    tq = min(TQ_ATTN, s)
    tk = min(TK_ATTN, tq)
    assert s % tq == 0 and tq % tk == 0 and tk % HEAD_DIM == 0
    nq, nk = s // tq, s // tk
    c_row = c_col.transpose(0, 2, 1).reshape(b * N_HEADS, nk, tk)
    kv_spec = pl.BlockSpec((1, s, HEAD_DIM), lambda bi, hi, i, tab: (bi, 0, hi))
    in_specs = [pl.BlockSpec((1, tq, HEAD_DIM), lambda bi, hi, i, tab: (bi, i, hi)),
                kv_spec, kv_spec,
                pl.BlockSpec((1, tq, N_HEADS), lambda bi, hi, i, tab: (bi, i, 0)),
                pl.BlockSpec((1, nk, tk), lambda bi, hi, i, tab: (bi * N_HEADS + hi, 0, 0))]
    out_spec = pl.BlockSpec((1, tq, HEAD_DIM), lambda bi, hi, i, tab: (bi, i, hi))
    static = dict(tq=tq, tk=tk, nq=nq, nk=nk, nbh=b * N_HEADS)

    def call(body, scratch, tab):
        return pl.pallas_call(
            functools.partial(body, **static),
            grid_spec=pltpu.PrefetchScalarGridSpec(num_scalar_prefetch=1, grid=(b, N_HEADS, nq), in_specs=in_specs,
                                                   out_specs=out_spec, scratch_shapes=scratch),
            out_shape=jax.ShapeDtypeStruct((b, s, ATTN_WIDTH), BF16),
            compiler_params=_params(("parallel", "parallel", "arbitrary")),
            name="prompt_attention",
        )(tab, q, k, v, c_col, c_row)

    def fixed_shift():
        tab = _skip_table(c_col, tq, tk, jnp.zeros((), F32), qk_bound + 1.0)
        return call(_flash_static_kernel,
                    [pltpu.VMEM((s, 2 * HEAD_DIM), BF16), pltpu.VMEM((tq, 2 * HEAD_DIM), F32),
                     pltpu.VMEM((tq, HEAD_DIM), F32)], tab)

    def running_max():
        tab = _skip_table(c_col, tq, tk, 2.0 * qk_bound + 2.0, jnp.zeros((), F32))
        return call(_flash_kernel,
                    [pltpu.VMEM((tq, 1), F32), pltpu.VMEM((tq, 1), F32), pltpu.VMEM((tq, HEAD_DIM), F32),
                     pltpu.VMEM((tq, 1), F32)], tab)

    return lax.cond(2.0 * qk_bound + 1.0 <= STATIC_SOFTMAX_RANGE, fixed_shift, running_max)


def _sample_attn_kernel(q_ref, ck_ref, cv_ref, kn_ref, vn_ref, cq_ref, cc_ref, cn_ref, o_ref,
                        m_sc, l_sc, acc_sc, *, nk, tk):
    j = pl.program_id(1)

    @pl.when(j == 0)
    def _():
        m_sc[...] = jnp.full(m_sc.shape, NEG, F32)
        l_sc[...] = jnp.zeros(l_sc.shape, F32)
        acc_sc[...] = jnp.zeros(acc_sc.shape, F32)

    def step(head_rows, k_ref, v_ref, crow_ref, masked):
        for hd in range(N_HEADS):
            sl = slice(hd * HEAD_DIM, (hd + 1) * HEAD_DIM)
            s = _qk(q_ref[0, :, sl], head_rows(k_ref, hd).astype(BF16))
            s = s + cq_ref[0, :, hd:hd + 1] - crow_ref[0, hd:hd + 1, :]
            if masked:
                s = _causal(s)
            m, l, acc = _softmax_step(s, head_rows(v_ref, hd).astype(BF16), m_sc[hd], l_sc[hd], acc_sc[hd])
            m_sc[hd] = m
            l_sc[hd] = l
            acc_sc[hd] = acc

    cached = lambda ref, hd: ref[0, pl.ds(hd, tk, stride=N_HEADS), :]
    fresh = lambda ref, hd: ref[0, :, hd * HEAD_DIM:(hd + 1) * HEAD_DIM]

    @pl.when(j < nk)
    def _():
        step(cached, ck_ref, cv_ref, cc_ref, False)

    @pl.when(j == nk)
    def _():
        step(fresh, kn_ref, vn_ref, cn_ref, True)
        for hd in range(N_HEADS):
            sl = slice(hd * HEAD_DIM, (hd + 1) * HEAD_DIM)
            o_ref[0, :, sl] = (acc_sc[hd] / l_sc[hd]).astype(o_ref.dtype)


def _sample_attention(q, cache_k, cache_v, k_new, v_new, cq_col, cc_row, cn_row):
    b, l, _ = q.shape
    past = cache_k.shape[1] // N_HEADS
    tk = min(TK_CACHE, past)
    assert past % tk == 0
    nk = past // tk
    cache_spec = pl.BlockSpec((1, tk * N_HEADS, HEAD_DIM), lambda bi, j: (bi, jnp.minimum(j, nk - 1), 0))
    new_spec = pl.BlockSpec((1, l, ATTN_WIDTH), lambda bi, j: (bi, 0, 0))
    return pl.pallas_call(
        functools.partial(_sample_attn_kernel, nk=nk, tk=tk),
        grid=(b, nk + 1),
        in_specs=[new_spec, cache_spec, cache_spec, new_spec, new_spec,
                  pl.BlockSpec((1, l, N_HEADS), lambda bi, j: (bi, 0, 0)),
                  pl.BlockSpec((1, N_HEADS, tk), lambda bi, j: (bi, 0, jnp.minimum(j, nk - 1))),
                  pl.BlockSpec((1, N_HEADS, l), lambda bi, j: (bi, 0, 0))],
        out_specs=new_spec,
        out_shape=jax.ShapeDtypeStruct((b, l, ATTN_WIDTH), BF16),
        scratch_shapes=[pltpu.VMEM((N_HEADS, l, 1), F32), pltpu.VMEM((N_HEADS, l, 1), F32),
                        pltpu.VMEM((N_HEADS, l, HEAD_DIM), F32)],
        compiler_params=_params(("parallel", "arbitrary")),
        name="sample_attention",
    )(q, cache_k, cache_v, k_new, v_new, cq_col, cc_row, cn_row)


def _finish_kernel(x_ref, po_ref, at_ref, wo_ref, gffn_ref, wrh_ref, wrl_ref, rb_ref, wsg_ref, wsu_ref, wsd_ref,
                   cin_ref,
                   base_ref, hp_ref, idx_ref, rank_ref, gate_ref, cnt_ref, carry_sc):
    @pl.when(pl.program_id(0) == 0)
    def _():
        carry_sc[...] = cin_ref[...]

    tm = x_ref.shape[0]
    x1 = x_ref[...] + _dot(po_ref[...], wo_ref[:POOL_WIDTH, :]) + _dot(at_ref[...], wo_ref[POOL_WIDTH:, :])
    ms = jnp.mean(x1 * x1, axis=-1, keepdims=True)
    h = x1 * lax.rsqrt(ms + EPS) * gffn_ref[...]
    hb = h.astype(BF16)

    g = _dot(hb, wsg_ref[...])
    up = _dot(hb, wsu_ref[...])
    act = (g * _sigmoid(g)) * up
    base_ref[...] = x1 + _dot(act.astype(BF16), wsd_ref[...])

    bits = lax.bitcast_convert_type(hb.astype(F32), U32)
    hp_ref[...] = (bits[:, :HALF] >> 16) | (bits[:, HALF:] & jnp.uint32(0xFFFF0000))

    h_lo = (h - hb.astype(F32)).astype(BF16)
    logits = _dot(hb, wrh_ref[...]) + (_dot(h_lo, wrh_ref[...]) + _dot(hb, wrl_ref[...]))
    scores = _sigmoid(logits)
    sel = scores + rb_ref[...]
    lane = lax.broadcasted_iota(jnp.int32, (tm, N_EXPERTS), 1).astype(F32)
    mask = jnp.zeros((tm, N_EXPERTS), F32)
    idx_cols, gate_cols = [], []
    for _ in range(TOP_K):
        mx = jnp.max(sel, axis=1, keepdims=True)
        ik = jnp.min(jnp.where(sel == mx, lane, float(N_EXPERTS)), axis=1, keepdims=True)
        oh = lane == ik
        gate_cols.append(jnp.sum(jnp.where(oh, scores, 0.0), axis=1, keepdims=True))
        idx_cols.append(ik)
        sel = jnp.where(oh, -jnp.inf, sel)
        mask = jnp.where(oh, 1.0, mask)

    r = lax.broadcasted_iota(jnp.int32, (tm, tm), 0)
    c = lax.broadcasted_iota(jnp.int32, (tm, tm), 1)
    tri = (r > c).astype(BF16)
    rank = _dot(tri, mask.astype(BF16)) + carry_sc[...]
    carry_sc[...] = carry_sc[...] + jnp.sum(mask, axis=0, keepdims=True)
    cnt_ref[...] = carry_sc[...]

    gsum = gate_cols[0]
    for gk in gate_cols[1:]:
        gsum = gsum + gk
    lane_k = lax.broadcasted_iota(jnp.int32, (tm, TOP_K), 1)
    idx_out = jnp.zeros((tm, TOP_K), F32)
    rank_out = jnp.zeros((tm, TOP_K), F32)
    gate_out = jnp.zeros((tm, TOP_K), F32)
    for k in range(TOP_K):
        rk = jnp.sum(jnp.where(lane == idx_cols[k], rank, 0.0), axis=1, keepdims=True)
        idx_out = jnp.where(lane_k == k, idx_cols[k], idx_out)
        rank_out = jnp.where(lane_k == k, rk, rank_out)
        gate_out = jnp.where(lane_k == k, gate_cols[k] / gsum * ROUTED_SCALE, gate_out)
    idx_ref[...] = idx_out.astype(jnp.int32)
    rank_ref[...] = rank_out.astype(jnp.int32)
    gate_ref[...] = gate_out


def _finish(x, pool_out, attn, w_o, g_ffn, wr_hi, wr_lo, router_bias, ws_gate, ws_up, ws_down, counts_in):
    t = x.shape[0]
    tm = min(TM_PROJ, t)
    assert t % tm == 0
    row = lambda w: pl.BlockSpec((tm, w), lambda i: (i, 0))
    return pl.pallas_call(
        _finish_kernel,
        grid=(t // tm,),
        in_specs=[row(D_MODEL), row(POOL_WIDTH), row(ATTN_WIDTH), _resident(w_o.shape), _resident((1, D_MODEL)),
                  _resident(wr_hi.shape), _resident(wr_lo.shape), _resident((1, N_EXPERTS)),
                  _resident(ws_gate.shape), _resident(ws_up.shape), _resident(ws_down.shape),
                  _resident((1, N_EXPERTS))],
        out_specs=[row(D_MODEL), row(HALF), row(TOP_K), row(TOP_K), row(TOP_K),
                   pl.BlockSpec((1, N_EXPERTS), lambda i: (0, 0))],
        out_shape=[jax.ShapeDtypeStruct((t, D_MODEL), F32),
                   jax.ShapeDtypeStruct((t, HALF), U32),
                   jax.ShapeDtypeStruct((t, TOP_K), jnp.int32),
                   jax.ShapeDtypeStruct((t, TOP_K), jnp.int32),
                   jax.ShapeDtypeStruct((t, TOP_K), F32),
                   jax.ShapeDtypeStruct((1, N_EXPERTS), F32)],
        scratch_shapes=[pltpu.VMEM((1, N_EXPERTS), F32)],
        compiler_params=_params(("arbitrary",)),
        name="outproj_route",
    )(x, pool_out, attn, w_o, g_ffn, wr_hi, wr_lo, router_bias, ws_gate, ws_up, ws_down, counts_in)


def _dispatch_kernel(dest_ref, hp_ref, *rest):
    xs_ref, sem = rest[-2:]
    tm = hp_ref.shape[0]

    def row_copy(r, d):
        return pltpu.make_async_copy(hp_ref.at[pl.ds(r, 1)], xs_ref.at[pl.ds(d, 1)], sem)

    def issue(r, carry):
        for k in range(TOP_K):
            row_copy(r, dest_ref[r * TOP_K + k]).start(priority=k % 2)
        return carry

    def drain(r, carry):
        for k in range(TOP_K):
            row_copy(r, dest_ref[r * TOP_K + k]).wait()
        return carry

    lax.fori_loop(0, tm, issue, 0)
    lax.fori_loop(0, tm, drain, 0)


def _dispatch(dest, hp, xs=None, n_rows=None):
    t = hp.shape[0]
    tm = min(TM_ROWS, t)
    assert t % tm == 0
    in_specs = [pl.BlockSpec((tm * TOP_K,), lambda i: (i,), memory_space=pltpu.SMEM),
                pl.BlockSpec((tm, HALF), lambda i: (i, 0))]
    args = [dest, hp]
    if xs is not None:
        in_specs.append(pl.BlockSpec(memory_space=pl.ANY))
        args.append(xs)
        n_rows = xs.shape[0]
    return pl.pallas_call(
        _dispatch_kernel,
        grid=(t // tm,),
        in_specs=in_specs,
        out_specs=pl.BlockSpec(memory_space=pl.ANY),
        out_shape=jax.ShapeDtypeStruct((n_rows, HALF), U32),
        scratch_shapes=[pltpu.SemaphoreType.DMA],
        input_output_aliases={2: 0} if xs is not None else {},
        compiler_params=_params(("arbitrary",)),
        name="moe_dispatch",
    )(*args)


def _unpack(p):
    lo = lax.bitcast_convert_type(p << 16, F32).astype(BF16)
    hi = lax.bitcast_convert_type(p & jnp.uint32(0xFFFF0000), F32).astype(BF16)
    return lo, hi


def _expert_kernel(blk_e_ref, nblk_ref, valid_ref, first_ref, slot_ref, next_ref,
                   xs_ref, wg_hbm, wu_hbm, wd_hbm, y_ref,
                   wg_f32, wu_f32, wd_f32, wg_sc, wu_sc, wd_sc, sems):
    b = pl.program_id(0)
    slot = slot_ref[b]

    def weight_copies(e, s):
        return (pltpu.make_async_copy(wg_hbm.at[e], wg_f32.at[s], sems.at[s, 0]),
                pltpu.make_async_copy(wu_hbm.at[e], wu_f32.at[s], sems.at[s, 1]),
                pltpu.make_async_copy(wd_hbm.at[e], wd_f32.at[s], sems.at[s, 2]))

    @pl.when(b == 0)
    def _():
        for cp in weight_copies(blk_e_ref[0], 0):
            cp.start()

    @pl.when(first_ref[b] == 1)
    def _():
        for cp in weight_copies(blk_e_ref[b], slot):
            cp.wait()

        @pl.when(next_ref[b] >= 0)
        def _():
            for cp in weight_copies(next_ref[b], 1 - slot):
                cp.start()

        wg_sc[...] = wg_f32[slot].astype(BF16)
        wu_sc[...] = wu_f32[slot].astype(BF16)
        wd_sc[...] = wd_f32[slot].astype(BF16)

    @pl.when(b < nblk_ref[0])
    def _():
        row = lax.broadcasted_iota(jnp.int32, (xs_ref.shape[0], 1), 0)
        lo, hi = _unpack(jnp.where(row < valid_ref[b], xs_ref[...], jnp.uint32(0)))
        g = _dot(lo, wg_sc[:HALF, :]) + _dot(hi, wg_sc[HALF:, :])
        up = _dot(lo, wu_sc[:HALF, :]) + _dot(hi, wu_sc[HALF:, :])
        act = (g * _sigmoid(g)) * up
        y_ref[...] = _dot(act.astype(BF16), wd_sc[...])


def _expert_tables(counts, pends, pstarts, nb, m):
    experts = jnp.arange(N_EXPERTS, dtype=jnp.int32)
    blk_row = jnp.arange(nb, dtype=jnp.int32) * m
    nblk = pends[-1] // m
    blk_e = jnp.minimum(jnp.sum((pends[None, :] <= blk_row[:, None]).astype(jnp.int32), axis=1), N_EXPERTS - 1)
    of_blk = blk_e[:, None] == experts
    pick = lambda table: jnp.sum(jnp.where(of_blk, table, 0), axis=1)
    blk_valid = jnp.clip(pick(counts) - (blk_row - pick(pstarts)), 0, m)
    prev_e = jnp.concatenate([jnp.full((1,), -1, jnp.int32), blk_e[:-1]])
    first = ((blk_row < nblk * m) & (blk_e != prev_e)).astype(jnp.int32)
    slot = (jnp.cumsum(first) - 1) % 2
    later_used = lax.cummin(jnp.where(counts > 0, experts, N_EXPERTS), axis=0, reverse=True)
    next_used = jnp.concatenate([later_used[1:], jnp.full((1,), N_EXPERTS, jnp.int32)])
    next_used = jnp.where(next_used < N_EXPERTS, next_used, -1)
    i32 = lambda a: a.astype(jnp.int32)
    return i32(blk_e), i32(nblk.reshape(1)), i32(blk_valid), first, i32(jnp.maximum(slot, 0)), i32(pick(next_used))


def _experts(tables, xs, w_gate, w_up, w_down):
    nb = tables[0].shape[0]
    m = M_EXPERT
    last = lambda b, nblk_ref: jnp.minimum(b, nblk_ref[0] - 1)
    grid_spec = pltpu.PrefetchScalarGridSpec(
        num_scalar_prefetch=len(tables),
        grid=(nb,),
        in_specs=[pl.BlockSpec((m, HALF), lambda b, be, nr, *_: (last(b, nr), 0)),
                  pl.BlockSpec(memory_space=pl.ANY), pl.BlockSpec(memory_space=pl.ANY),
                  pl.BlockSpec(memory_space=pl.ANY)],
        out_specs=pl.BlockSpec((m, D_MODEL), lambda b, be, nr, *_: (last(b, nr), 0)),
        scratch_shapes=[pltpu.VMEM((2, D_MODEL, D_EXPERT), F32), pltpu.VMEM((2, D_MODEL, D_EXPERT), F32),
                        pltpu.VMEM((2, D_EXPERT, D_MODEL), F32),
                        pltpu.VMEM((D_MODEL, D_EXPERT), BF16), pltpu.VMEM((D_MODEL, D_EXPERT), BF16),
                        pltpu.VMEM((D_EXPERT, D_MODEL), BF16),
                        pltpu.SemaphoreType.DMA((2, 3))],
    )
    return pl.pallas_call(
        _expert_kernel,
        grid_spec=grid_spec,
        out_shape=jax.ShapeDtypeStruct((nb * m, D_MODEL), F32),
        compiler_params=_params(("arbitrary",)),
        name="moe_experts",
    )(*tables, xs, w_gate, w_up, w_down)


def _combine_kernel(dest_ref, base_ref, gate_ref, ys_ref, o_ref, buf, sem):
    tm = base_ref.shape[0]

    def row_copy(r, k, d):
        return pltpu.make_async_copy(ys_ref.at[pl.ds(d, 1)], buf.at[k, pl.ds(r, 1)], sem)

    def issue(r, carry):
        for k in range(TOP_K):
            row_copy(r, k, dest_ref[r * TOP_K + k]).start(priority=k % 2)
        return carry

    def drain(r, carry):
        for k in range(TOP_K):
            row_copy(r, k, dest_ref[r * TOP_K + k]).wait()
        return carry

    lax.fori_loop(0, tm, issue, 0)
    lax.fori_loop(0, tm, drain, 0)
    for r0 in range(0, tm, COMBINE_ROWS):
        rows = slice(r0, r0 + COMBINE_ROWS)
        acc = base_ref[rows, :]
        for k in range(TOP_K):
            acc = acc + gate_ref[rows, k:k + 1] * buf[k, rows, :]
        o_ref[rows, :] = acc


def _combine(dest, base, gates, ys):
    t = base.shape[0]
    tm = min(TM_ROWS, t)
    assert t % tm == 0
    return pl.pallas_call(
        _combine_kernel,
        grid=(t // tm,),
        in_specs=[pl.BlockSpec((tm * TOP_K,), lambda i: (i,), memory_space=pltpu.SMEM),
                  pl.BlockSpec((tm, D_MODEL), lambda i: (i, 0)),
                  pl.BlockSpec((tm, TOP_K), lambda i: (i, 0)),
                  pl.BlockSpec(memory_space=pl.ANY)],
        out_specs=pl.BlockSpec((tm, D_MODEL), lambda i: (i, 0)),
        out_shape=jax.ShapeDtypeStruct((t, D_MODEL), F32),
        scratch_shapes=[pltpu.VMEM((TOP_K, tm, D_MODEL), F32), pltpu.SemaphoreType.DMA],
        compiler_params=_params(("arbitrary",)),
        name="moe_combine",
    )(dest, base, gates, ys)


def _mixer(x, prefix, start_pos, w, init_c):
    b, l, _ = x.shape
    u, q, k, v, logf, kb, vb = _inproj(x.reshape(b * l, D_MODEL), w["g_mix"], w["w_main"], w["w_f"], w["b_f"],
                                       w["g_q"], w["g_k"])
    u = u.reshape(b, l, POOL_WIDTH)
    pool_out = _pool(u, prefix, w["w_pool"], w["pool_scale"], start_pos)
    c = _cumsum(logf.reshape(b, l, N_HEADS), init_c)
    shp = (b, l, ATTN_WIDTH)
    return u, q.reshape(shp), k.reshape(shp), v.reshape(shp), logf, pool_out, c, kb.reshape(shp), vb.reshape(shp)


def _slot_starts(idx, pstarts):
    experts = jnp.arange(N_EXPERTS, dtype=jnp.int32)
    return jnp.sum(jnp.where(idx[..., None] == experts, pstarts, 0), axis=-1)


def kernel(x_prompt, x_sample, cache_k, cache_v, cache_logf, cache_pool, g_mix, w_in, b_f, g_q, g_k, w_pool, pool_scale, w_o, g_ffn, w_router, router_bias, w_gate, w_up, w_down, ws_gate, ws_up, ws_down):
    depth = w_in.shape[0]
    assert depth == 1, "single-layer step"
    bp, sp, _ = x_prompt.shape
    bs, ls, _ = x_sample.shape
    past = cache_k.shape[2]
    assert sp >= POOL_BUF and ls >= POOL_BUF
    n_main = POOL_WIDTH + 3 * ATTN_WIDTH

    w = dict(
        g_mix=g_mix[0][None], g_q=g_q[0][None], g_k=g_k[0][None], b_f=b_f[0][None],
        w_main=w_in[0, :, :n_main].astype(BF16),
        w_f=jnp.pad(w_in[0, :, n_main:], ((0, 0), (0, HEAD_DIM - N_HEADS))).astype(BF16),
        w_pool=w_pool[0].astype(BF16), pool_scale=pool_scale[0][None],
    )
    wr_hi = w_router[0].astype(BF16)
    wr_lo = (w_router[0] - wr_hi.astype(F32)).astype(BF16)
    fin = (w_o[0].astype(BF16), g_ffn[0][None], wr_hi, wr_lo, router_bias[0][None],
           ws_gate[0].astype(BF16), ws_up[0].astype(BF16), ws_down[0].astype(BF16))
    qk_bound = 1.02 * HEAD_DIM ** 0.5 * jnp.max(jnp.abs(g_q[0])) * jnp.max(jnp.abs(g_k[0]))

    zero_c = lambda b: jnp.zeros((b, 1, N_HEADS), F32)

    u_p, q_p, k_p, v_p, logf_p, pool_p, c_p, kb_p, vb_p = _mixer(
        x_prompt, jnp.zeros((bp, POOL_HALO, POOL_WIDTH), F32), 0, w, zero_c(bp))
    attn_p = _prompt_attention(q_p, kb_p, vb_p, c_p, qk_bound)

    ck = cache_k[0].reshape(bs, past * N_HEADS, HEAD_DIM)
    cv = cache_v[0].reshape(bs, past * N_HEADS, HEAD_DIM)
    c_cache = _cumsum(cache_logf[0], zero_c(bs))
    prefix_s = jnp.pad(cache_pool[0], ((0, 0), (POOL_HALO - POOL_BUF, 0), (0, 0)))
    u_s, q_s, k_s, v_s, logf_s, pool_s, c_s, _, _ = _mixer(x_sample, prefix_s, past, w,
                                                           c_cache[:, past - 1:past, :])
    attn_s = _sample_attention(q_s, ck, cv, k_s, v_s, c_s, c_cache.transpose(0, 2, 1), c_s.transpose(0, 2, 1))

    tp, ts = bp * sp, bs * ls
    base_p, hp_p, idx_p, rank_p, gate_p, cnt_p = _finish(
        x_prompt.reshape(tp, D_MODEL), pool_p.reshape(tp, POOL_WIDTH), attn_p.reshape(tp, ATTN_WIDTH), *fin,
        jnp.zeros((1, N_EXPERTS), F32))
    base_s, hp_s, idx_s, rank_s, gate_s, cnt_s = _finish(
        x_sample.reshape(ts, D_MODEL), pool_s.reshape(ts, POOL_WIDTH), attn_s.reshape(ts, ATTN_WIDTH), *fin, cnt_p)

    m = M_EXPERT
    counts = cnt_s[0].astype(jnp.int32)
    pcounts = (counts + m - 1) // m * m
    pends = jnp.cumsum(pcounts)
    pstarts = pends - pcounts
    nb = -(-(tp + ts) * TOP_K // m) + N_EXPERTS
    tables = _expert_tables(counts, pends, pstarts, nb, m)
    dest_p = (_slot_starts(idx_p, pstarts) + rank_p).reshape(tp * TOP_K)
    dest_s = (_slot_starts(idx_s, pstarts) + rank_s).reshape(ts * TOP_K)

    xs = _dispatch(dest_p, hp_p, n_rows=nb * m)
    xs = _dispatch(dest_s, hp_s, xs)
    ys = _experts(tables, xs, w_gate[0], w_up[0], w_down[0])
    y_p = _combine(dest_p, base_p, gate_p, ys).reshape(bp, sp, D_MODEL)
    y_s = _combine(dest_s, base_s, gate_s, ys).reshape(bs, ls, D_MODEL)

    heads = lambda a, b, l: a.reshape(1, b, l, N_HEADS, HEAD_DIM)
    return (y_p, y_s,
            heads(k_p, bp, sp), heads(v_p, bp, sp), logf_p.reshape(1, bp, sp, N_HEADS), u_p[None, :, sp - POOL_BUF:, :],
            heads(k_s, bs, ls), heads(v_s, bs, ls), logf_s.reshape(1, bs, ls, N_HEADS), u_s[None, :, ls - POOL_BUF:, :])
```

```python
import functools

import jax
import jax.numpy as jnp
from jax import lax
from jax.experimental import pallas as pl
from jax.experimental.pallas import tpu as pltpu

D_MODEL = 2048
POOL_WIDTH = 1024
ATTN_WIDTH = 1024
HEAD_DIM = 128
N_HEADS = 8
POOL_WINDOWS = (2, 4, 8, 16)
POOL_GROUP = 256
POOL_BUF = 15
POOL_HALO = 16
N_EXPERTS = 64
TOP_K = 8
D_EXPERT = 512
D_SHARED = 512
ROUTED_SCALE = 2.5
EPS = 1e-6
NEG = -1e30

F32 = jnp.float32
BF16 = jnp.bfloat16
U32 = jnp.uint32
HALF = D_MODEL // 2

V7X_VMEM_LIMIT = 56 * 1024 * 1024

TM_PROJ = 256
TM_FINISH = 512
TL_SCAN = 512
TL_POOL = 512
TQ_ATTN = 512
TK_ATTN = 256
EXP_UNDERFLOW = 110.0
STATIC_SOFTMAX_RANGE = 60.0
TK_CACHE = 1024
TM_ROWS = 128
M_EXPERT = 256
COMBINE_ROWS = 8


def _params(sem, vmem=V7X_VMEM_LIMIT):
    return pltpu.CompilerParams(dimension_semantics=sem, vmem_limit_bytes=vmem)


def _resident(shape):
    zeros = (0,) * len(shape)
    return pl.BlockSpec(shape, lambda *_: zeros, pipeline_mode=pl.Buffered(1))


def _dot(a, b):
    return jnp.dot(a, b, preferred_element_type=F32)


def _sigmoid(x):
    return 1.0 / (1.0 + jnp.exp(-x))


def _pack_rows(xb):
    bits = lax.bitcast_convert_type(xb.astype(F32), U32)
    return (bits[:, :HALF] >> 16) | (bits[:, HALF:] & jnp.uint32(0xFFFF0000))


def _unpack_rows_f32(p):
    return (lax.bitcast_convert_type(p << 16, F32),
            lax.bitcast_convert_type(p & jnp.uint32(0xFFFF0000), F32))


def _inproj_kernel(x_ref, gmix_ref, w_ref, wf_ref, bf_ref, gq_ref, gk_ref,
                   u_ref, q_ref, k_ref, v_ref, logf_ref, kb_ref, vb_ref):
    x = x_ref[...]
    ms = jnp.mean(x * x, axis=-1, keepdims=True)
    h = (x * lax.rsqrt(ms + EPS) * gmix_ref[...]).astype(BF16)
    p, a = POOL_WIDTH, ATTN_WIDTH
    u_ref[...] = _dot(h, w_ref[:, :p])
    zq = _dot(h, w_ref[:, p:p + a])
    zk = _dot(h, w_ref[:, p + a:p + 2 * a])
    v = _dot(h, w_ref[:, p + 2 * a:p + 3 * a])
    v_ref[...] = v
    vb_ref[...] = v.astype(BF16)
    scale = HEAD_DIM ** -0.5
    for hd in range(N_HEADS):
        sl = slice(hd * HEAD_DIM, (hd + 1) * HEAD_DIM)
        qh = zq[:, sl]
        qn = qh * lax.rsqrt(jnp.mean(qh * qh, axis=-1, keepdims=True) + EPS) * gq_ref[...]
        q_ref[:, sl] = (qn * scale).astype(BF16)
        kh = zk[:, sl]
        kn = kh * lax.rsqrt(jnp.mean(kh * kh, axis=-1, keepdims=True) + EPS) * gk_ref[...]
        k_ref[:, sl] = kn
        kb_ref[:, sl] = kn.astype(BF16)
    zf = _dot(h, wf_ref[...])[:, :N_HEADS] + bf_ref[...]
    logf_ref[...] = jnp.minimum(zf, 0.0) - jnp.log1p(jnp.exp(-jnp.abs(zf)))


def _inproj(x, g_mix, w_main, w_f, b_f, g_q, g_k):
    t = x.shape[0]
    tm = min(TM_PROJ, t)
    assert t % tm == 0
    row = lambda w: pl.BlockSpec((tm, w), lambda i: (i, 0))
    return pl.pallas_call(
        _inproj_kernel,
        grid=(t // tm,),
        in_specs=[row(D_MODEL), _resident((1, D_MODEL)), _resident(w_main.shape), _resident(w_f.shape),
                  _resident((1, N_HEADS)), _resident((1, HEAD_DIM)), _resident((1, HEAD_DIM))],
        out_specs=[row(POOL_WIDTH), row(ATTN_WIDTH), row(ATTN_WIDTH), row(ATTN_WIDTH), row(N_HEADS),
                   row(ATTN_WIDTH), row(ATTN_WIDTH)],
        out_shape=[jax.ShapeDtypeStruct((t, POOL_WIDTH), F32),
                   jax.ShapeDtypeStruct((t, ATTN_WIDTH), BF16),
                   jax.ShapeDtypeStruct((t, ATTN_WIDTH), F32),
                   jax.ShapeDtypeStruct((t, ATTN_WIDTH), F32),
                   jax.ShapeDtypeStruct((t, N_HEADS), F32),
                   jax.ShapeDtypeStruct((t, ATTN_WIDTH), BF16),
                   jax.ShapeDtypeStruct((t, ATTN_WIDTH), BF16)],
        compiler_params=_params(("arbitrary",)),
        name="inproj",
    )(x, g_mix, w_main, w_f, b_f, g_q, g_k)


def _cumsum_kernel(x_ref, init_ref, o_ref, carry_ref):
    @pl.when(pl.program_id(1) == 0)
    def _():
        carry_ref[...] = init_ref[0]

    x = x_ref[0]
    tl = x.shape[0]
    r = lax.broadcasted_iota(jnp.int32, (tl, tl), 0)
    c = lax.broadcasted_iota(jnp.int32, (tl, tl), 1)
    tri = (r >= c).astype(F32)
    cs = jnp.dot(tri, x, precision=lax.Precision.HIGHEST, preferred_element_type=F32) + carry_ref[...]
    o_ref[0] = cs
    carry_ref[...] = cs[tl - 1:tl, :]


def _cumsum(x, init):
    b, l, h = x.shape
    tl = min(TL_SCAN, l)
    assert l % tl == 0
    return pl.pallas_call(
        _cumsum_kernel,
        grid=(b, l // tl),
        in_specs=[pl.BlockSpec((1, tl, h), lambda bi, li: (bi, li, 0)),
                  pl.BlockSpec((1, 1, h), lambda bi, li: (bi, 0, 0))],
        out_specs=pl.BlockSpec((1, tl, h), lambda bi, li: (bi, li, 0)),
        out_shape=jax.ShapeDtypeStruct((b, l, h), F32),
        scratch_shapes=[pltpu.VMEM((1, h), F32)],
        compiler_params=_params(("arbitrary", "arbitrary")),
        name="logf_cumsum",
    )(x, init)


def _pool_kernel(u_ref, prefix_ref, wp_ref, ps_ref, o_ref, ext_ref, *, start_pos, tl):
    l = pl.program_id(1)

    @pl.when(l == 0)
    def _():
        ext_ref[0:POOL_HALO, :] = prefix_ref[0]

    u = u_ref[0]
    ext_ref[POOL_HALO:POOL_HALO + tl, :] = u
    pos = start_pos + l * tl + lax.broadcasted_iota(jnp.int32, (tl, 1), 0)
    for g, w in enumerate(POOL_WINDOWS):
        cols = slice(g * POOL_GROUP, (g + 1) * POOL_GROUP)
        s = u[:, cols]
        for j in range(1, w):
            s = s + ext_ref[POOL_HALO - j:POOL_HALO - j + tl, cols]
        cnt = jnp.minimum(w, pos + 1).astype(F32)
        d = s / cnt - u[:, cols]
        og = _dot(d.astype(BF16), wp_ref[g]) * ps_ref[:, cols]
        o_ref[0, :, cols] = og.astype(o_ref.dtype)
    ext_ref[0:POOL_HALO, :] = ext_ref[tl:tl + POOL_HALO, :]


def _pool(u, prefix, w_pool, pool_scale, start_pos):
    b, l, p = u.shape
    tl = min(TL_POOL, l)
    assert l % tl == 0 and tl >= POOL_HALO
    return pl.pallas_call(
        functools.partial(_pool_kernel, start_pos=start_pos, tl=tl),
        grid=(b, l // tl),
        in_specs=[pl.BlockSpec((1, tl, p), lambda bi, li: (bi, li, 0)),
                  pl.BlockSpec((1, POOL_HALO, p), lambda bi, li: (bi, 0, 0)),
                  _resident(w_pool.shape), _resident((1, p))],
        out_specs=pl.BlockSpec((1, tl, p), lambda bi, li: (bi, li, 0)),
        out_shape=jax.ShapeDtypeStruct((b, l, p), BF16),
        scratch_shapes=[pltpu.VMEM((tl + POOL_HALO, p), F32)],
        compiler_params=_params(("arbitrary", "arbitrary")),
        name="pool_mix",
    )(u, prefix, w_pool, pool_scale)


def _softmax_step(s, v, m_prev, l_prev, acc_prev):
    m_new = jnp.maximum(m_prev, jnp.max(s, axis=1, keepdims=True))
    alpha = jnp.exp(m_prev - m_new)
    p = jnp.exp(s - m_new)
    l_new = alpha * l_prev + jnp.sum(p, axis=1, keepdims=True)
    acc_new = alpha * acc_prev + _dot(p.astype(BF16), v)
    return m_new, l_new, acc_new


def _qk(q, k):
    return lax.dot_general(q, k, (((1,), (1,)), ((), ())), preferred_element_type=F32)


def _causal(s):
    r = lax.broadcasted_iota(jnp.int32, s.shape, 0)
    c = lax.broadcasted_iota(jnp.int32, s.shape, 1)
    return jnp.where(r >= c, s, NEG)


def _flash_kernel(tab_ref, q_ref, k_ref, v_ref, cq_ref, ck_ref, o_ref, m_sc, l_sc, acc_sc, cq_sc,
                  *, tq, tk, nq, nk, nbh):
    hd = pl.program_id(1)
    i = pl.program_id(2)
    bh = pl.program_id(0) * N_HEADS + hd
    r = tq // tk

    m_sc[...] = jnp.full(m_sc.shape, NEG, F32)
    l_sc[...] = jnp.zeros(l_sc.shape, F32)
    acc_sc[...] = jnp.zeros(acc_sc.shape, F32)
    cq = cq_ref[0]
    lane = lax.broadcasted_iota(jnp.int32, cq.shape, 1)
    cq_sc[...] = jnp.sum(jnp.where(lane == hd, cq, 0.0), axis=1, keepdims=True)
    q = q_ref[0]

    def tile(jj, masked):
        off = pl.multiple_of(jj * tk, tk)
        s = _qk(q, k_ref[0, pl.ds(off, tk), :]) + cq_sc[...] - ck_ref[0, pl.ds(jj, 1), :]
        if masked:
            rr = lax.broadcasted_iota(jnp.int32, s.shape, 0) + i * tq
            cc = lax.broadcasted_iota(jnp.int32, s.shape, 1) + off
            s = jnp.where(rr >= cc, s, NEG)
        m, l, acc = _softmax_step(s, v_ref[0, pl.ds(off, tk), :], m_sc[...], l_sc[...], acc_sc[...])
        m_sc[...] = m
        l_sc[...] = l
        acc_sc[...] = acc

    for d in range(r):
        tile(i * r + d, True)

    bound = tab_ref[0]
    cmax_i = tab_ref[2 + bh * nq + i]
    cmin_base = 2 + nbh * nq + bh * nk

    def live(jj):
        cmin_j = tab_ref[cmin_base + jnp.maximum(jj, 0)]
        return (jj >= 0) & (bound + cmax_i - cmin_j >= -EXP_UNDERFLOW)

    def body(jj):
        tile(jj, False)
        return jj - 1

    lax.while_loop(live, body, i * r - 1)
    o_ref[0] = (acc_sc[...] / l_sc[...]).astype(o_ref.dtype)


def _flash_static_kernel(tab_ref, q_ref, k_ref, v_ref, cq_ref, ck_ref, o_ref, vaug_sc, acc_sc, cqb_sc,
                         *, tq, tk, nq, nk, nbh):
    hd = pl.program_id(1)
    i = pl.program_id(2)
    bh = pl.program_id(0) * N_HEADS + hd
    r = tq // tk

    @pl.when(i == 0)
    def _():
        vaug_sc[:, :HEAD_DIM] = v_ref[0]
        vaug_sc[:, HEAD_DIM:] = jnp.ones((vaug_sc.shape[0], HEAD_DIM), BF16)

    cq = cq_ref[0]
    lane = lax.broadcasted_iota(jnp.int32, cq.shape, 1)
    col = jnp.sum(jnp.where(lane == hd, cq, 0.0), axis=1, keepdims=True) - tab_ref[1]
    cqb_sc[...] = jnp.broadcast_to(col, cqb_sc.shape)
    acc_sc[...] = jnp.zeros(acc_sc.shape, F32)
    q = q_ref[0]

    def tile(jj, masked):
        off = pl.multiple_of(jj * tk, tk)
        s = _qk(q, k_ref[0, pl.ds(off, tk), :])
        ck = ck_ref[0, pl.ds(jj, 1), :]
        parts = []
        for c in range(tk // HEAD_DIM):
            sl = slice(c * HEAD_DIM, (c + 1) * HEAD_DIM)
            sc = s[:, sl] + cqb_sc[...] - ck[:, sl]
            if masked:
                rr = lax.broadcasted_iota(jnp.int32, sc.shape, 0) + i * tq
                cc = lax.broadcasted_iota(jnp.int32, sc.shape, 1) + (off + c * HEAD_DIM)
                sc = jnp.where(rr >= cc, sc, NEG)
            parts.append(jnp.exp(sc).astype(BF16))
        acc_sc[...] += _dot(jnp.concatenate(parts, axis=1), vaug_sc[pl.ds(off, tk), :])

    for d in range(r):
        tile(i * r + d, True)

    cmax_i = tab_ref[2 + bh * nq + i]
    cmin_base = 2 + nbh * nq + bh * nk

    def live(jj):
        cmin_j = tab_ref[cmin_base + jnp.maximum(jj, 0)]
        return (jj >= 0) & (tab_ref[0] + cmax_i - cmin_j >= -EXP_UNDERFLOW)

    def body(jj):
        tile(jj, False)
        return jj - 1

    lax.while_loop(live, body, i * r - 1)
    o_ref[0] = (acc_sc[:, :HEAD_DIM] / acc_sc[:, HEAD_DIM:]).astype(o_ref.dtype)


def _skip_table(c, tq, tk, slack, shift):
    b, s, h = c.shape
    cmax = c.reshape(b, s // tq, tq, h).max(axis=2)
    cmin = lax.cummin(c.reshape(b, s // tk, tk, h).min(axis=2), axis=1)
    flat = lambda a: a.transpose(0, 2, 1).reshape(-1)
    return jnp.concatenate([slack.reshape(1), shift.reshape(1), flat(cmax), flat(cmin)]).astype(F32)


def _prompt_attention(q, k, v, c_col, qk_bound):
    b, s, _ = q.shape
    tq = min(TQ_ATTN, s)
    tk = min(TK_ATTN, tq)
    assert s % tq == 0 and tq % tk == 0 and tk % HEAD_DIM == 0
    nq, nk = s // tq, s // tk
    c_row = c_col.transpose(0, 2, 1).reshape(b * N_HEADS, nk, tk)
    kv_spec = pl.BlockSpec((1, s, HEAD_DIM), lambda bi, hi, i, tab: (bi, 0, hi))
    in_specs = [pl.BlockSpec((1, tq, HEAD_DIM), lambda bi, hi, i, tab: (bi, i, hi)),
                kv_spec, kv_spec,
                pl.BlockSpec((1, tq, N_HEADS), lambda bi, hi, i, tab: (bi, i, 0)),
                pl.BlockSpec((1, nk, tk), lambda bi, hi, i, tab: (bi * N_HEADS + hi, 0, 0))]
    out_spec = pl.BlockSpec((1, tq, HEAD_DIM), lambda bi, hi, i, tab: (bi, i, hi))
    static = dict(tq=tq, tk=tk, nq=nq, nk=nk, nbh=b * N_HEADS)

    def call(body, scratch, tab):
        return pl.pallas_call(
            functools.partial(body, **static),
            grid_spec=pltpu.PrefetchScalarGridSpec(num_scalar_prefetch=1, grid=(b, N_HEADS, nq), in_specs=in_specs,
                                                   out_specs=out_spec, scratch_shapes=scratch),
            out_shape=jax.ShapeDtypeStruct((b, s, ATTN_WIDTH), BF16),
            compiler_params=_params(("parallel", "parallel", "arbitrary")),
            name="prompt_attention",
        )(tab, q, k, v, c_col, c_row)

    def fixed_shift():
        tab = _skip_table(c_col, tq, tk, jnp.zeros((), F32), qk_bound + 1.0)
        return call(_flash_static_kernel,
                    [pltpu.VMEM((s, 2 * HEAD_DIM), BF16), pltpu.VMEM((tq, 2 * HEAD_DIM), F32),
                     pltpu.VMEM((tq, HEAD_DIM), F32)], tab)

    def running_max():
        tab = _skip_table(c_col, tq, tk, 2.0 * qk_bound + 2.0, jnp.zeros((), F32))
        return call(_flash_kernel,
                    [pltpu.VMEM((tq, 1), F32), pltpu.VMEM((tq, 1), F32), pltpu.VMEM((tq, HEAD_DIM), F32),
                     pltpu.VMEM((tq, 1), F32)], tab)

    return lax.cond(2.0 * qk_bound + 1.0 <= STATIC_SOFTMAX_RANGE, fixed_shift, running_max)


def _sample_attn_kernel(q_ref, ck_ref, cv_ref, kn_ref, vn_ref, cq_ref, cc_ref, cn_ref, o_ref,
                        m_sc, l_sc, acc_sc, *, nk, tk):
    j = pl.program_id(1)

    @pl.when(j == 0)
    def _():
        m_sc[...] = jnp.full(m_sc.shape, NEG, F32)
        l_sc[...] = jnp.zeros(l_sc.shape, F32)
        acc_sc[...] = jnp.zeros(acc_sc.shape, F32)

    def step(head_rows, k_ref, v_ref, crow_ref, masked):
        for hd in range(N_HEADS):
            sl = slice(hd * HEAD_DIM, (hd + 1) * HEAD_DIM)
            s = _qk(q_ref[0, :, sl], head_rows(k_ref, hd).astype(BF16))
            s = s + cq_ref[0, :, hd:hd + 1] - crow_ref[0, hd:hd + 1, :]
            if masked:
                s = _causal(s)
            m, l, acc = _softmax_step(s, head_rows(v_ref, hd).astype(BF16), m_sc[hd], l_sc[hd], acc_sc[hd])
            m_sc[hd] = m
            l_sc[hd] = l
            acc_sc[hd] = acc

    cached = lambda ref, hd: ref[0, pl.ds(hd, tk, stride=N_HEADS), :]
    fresh = lambda ref, hd: ref[0, :, hd * HEAD_DIM:(hd + 1) * HEAD_DIM]

    @pl.when(j < nk)
    def _():
        step(cached, ck_ref, cv_ref, cc_ref, False)

    @pl.when(j == nk)
    def _():
        step(fresh, kn_ref, vn_ref, cn_ref, True)
        for hd in range(N_HEADS):
            sl = slice(hd * HEAD_DIM, (hd + 1) * HEAD_DIM)
            o_ref[0, :, sl] = (acc_sc[hd] / l_sc[hd]).astype(o_ref.dtype)


def _sample_attention(q, cache_k, cache_v, k_new, v_new, cq_col, cc_row, cn_row):
    b, l, _ = q.shape
    past = cache_k.shape[1] // N_HEADS
    tk = min(TK_CACHE, past)
    assert past % tk == 0
    nk = past // tk
    cache_spec = pl.BlockSpec((1, tk * N_HEADS, HEAD_DIM), lambda bi, j: (bi, jnp.minimum(j, nk - 1), 0))
    new_spec = pl.BlockSpec((1, l, ATTN_WIDTH), lambda bi, j: (bi, 0, 0))
    return pl.pallas_call(
        functools.partial(_sample_attn_kernel, nk=nk, tk=tk),
        grid=(b, nk + 1),
        in_specs=[new_spec, cache_spec, cache_spec, new_spec, new_spec,
                  pl.BlockSpec((1, l, N_HEADS), lambda bi, j: (bi, 0, 0)),
                  pl.BlockSpec((1, N_HEADS, tk), lambda bi, j: (bi, 0, jnp.minimum(j, nk - 1))),
                  pl.BlockSpec((1, N_HEADS, l), lambda bi, j: (bi, 0, 0))],
        out_specs=new_spec,
        out_shape=jax.ShapeDtypeStruct((b, l, ATTN_WIDTH), BF16),
        scratch_shapes=[pltpu.VMEM((N_HEADS, l, 1), F32), pltpu.VMEM((N_HEADS, l, 1), F32),
                        pltpu.VMEM((N_HEADS, l, HEAD_DIM), F32)],
        compiler_params=_params(("parallel", "arbitrary")),
        name="sample_attention",
    )(q, cache_k, cache_v, k_new, v_new, cq_col, cc_row, cn_row)


def _finish_kernel(x_ref, po_ref, at_ref, wo_ref, gffn_ref, wrh_ref, wrl_ref, rb_ref, wsg_ref, wsu_ref, wsd_ref,
                   cin_ref,
                   base_ref, hp_ref, idx_ref, rank_ref, gate_ref, cnt_ref, carry_sc):
    @pl.when(pl.program_id(0) == 0)
    def _():
        carry_sc[...] = cin_ref[...]

    tm = x_ref.shape[0]
    x1 = x_ref[...] + _dot(po_ref[...], wo_ref[:POOL_WIDTH, :]) + _dot(at_ref[...], wo_ref[POOL_WIDTH:, :])
    ms = jnp.mean(x1 * x1, axis=-1, keepdims=True)
    h = x1 * lax.rsqrt(ms + EPS) * gffn_ref[...]
    hb = h.astype(BF16)

    g = _dot(hb, wsg_ref[...])
    up = _dot(hb, wsu_ref[...])
    act = (g * _sigmoid(g)) * up
    base_ref[...] = x1 + _dot(act.astype(BF16), wsd_ref[...])

    hp_ref[...] = _pack_rows(hb)

    h_lo = (h - hb.astype(F32)).astype(BF16)
    logits = _dot(hb, wrh_ref[...]) + (_dot(h_lo, wrh_ref[...]) + _dot(hb, wrl_ref[...]))
    scores = _sigmoid(logits)
    sel = scores + rb_ref[...]
    lane = lax.broadcasted_iota(jnp.int32, (tm, N_EXPERTS), 1).astype(F32)
    mask = jnp.zeros((tm, N_EXPERTS), F32)
    idx_cols, gate_cols = [], []
    for _ in range(TOP_K):
        mx = jnp.max(sel, axis=1, keepdims=True)
        ik = jnp.min(jnp.where(sel == mx, lane, float(N_EXPERTS)), axis=1, keepdims=True)
        oh = lane == ik
        gate_cols.append(jnp.sum(jnp.where(oh, scores, 0.0), axis=1, keepdims=True))
        idx_cols.append(ik)
        sel = jnp.where(oh, -jnp.inf, sel)
        mask = jnp.where(oh, 1.0, mask)

    r = lax.broadcasted_iota(jnp.int32, (tm, tm), 0)
    c = lax.broadcasted_iota(jnp.int32, (tm, tm), 1)
    tri = (r > c).astype(BF16)
    rank = _dot(tri, mask.astype(BF16)) + carry_sc[...]
    carry_sc[...] = carry_sc[...] + jnp.sum(mask, axis=0, keepdims=True)
    cnt_ref[...] = carry_sc[...]

    gsum = gate_cols[0]
    for gk in gate_cols[1:]:
        gsum = gsum + gk
    lane_k = lax.broadcasted_iota(jnp.int32, (tm, TOP_K), 1)
    idx_out = jnp.zeros((tm, TOP_K), F32)
    rank_out = jnp.zeros((tm, TOP_K), F32)
    gate_out = jnp.zeros((tm, TOP_K), F32)
    for k in range(TOP_K):
        rk = jnp.sum(jnp.where(lane == idx_cols[k], rank, 0.0), axis=1, keepdims=True)
        idx_out = jnp.where(lane_k == k, idx_cols[k], idx_out)
        rank_out = jnp.where(lane_k == k, rk, rank_out)
        gate_out = jnp.where(lane_k == k, gate_cols[k] / gsum * ROUTED_SCALE, gate_out)
    idx_ref[...] = idx_out.astype(jnp.int32)
    rank_ref[...] = rank_out.astype(jnp.int32)
    gate_ref[...] = gate_out


def _finish(x, pool_out, attn, w_o, g_ffn, wr_hi, wr_lo, router_bias, ws_gate, ws_up, ws_down, counts_in):
    t = x.shape[0]
    tm = min(TM_FINISH, t)
    assert t % tm == 0
    row = lambda w: pl.BlockSpec((tm, w), lambda i: (i, 0))
    return pl.pallas_call(
        _finish_kernel,
        grid=(t // tm,),
        in_specs=[row(D_MODEL), row(POOL_WIDTH), row(ATTN_WIDTH), _resident(w_o.shape), _resident((1, D_MODEL)),
                  _resident(wr_hi.shape), _resident(wr_lo.shape), _resident((1, N_EXPERTS)),
                  _resident(ws_gate.shape), _resident(ws_up.shape), _resident(ws_down.shape),
                  _resident((1, N_EXPERTS))],
        out_specs=[row(D_MODEL), row(HALF), row(TOP_K), row(TOP_K), row(TOP_K),
                   pl.BlockSpec((1, N_EXPERTS), lambda i: (0, 0))],
        out_shape=[jax.ShapeDtypeStruct((t, D_MODEL), F32),
                   jax.ShapeDtypeStruct((t, HALF), U32),
                   jax.ShapeDtypeStruct((t, TOP_K), jnp.int32),
                   jax.ShapeDtypeStruct((t, TOP_K), jnp.int32),
                   jax.ShapeDtypeStruct((t, TOP_K), F32),
                   jax.ShapeDtypeStruct((1, N_EXPERTS), F32)],
        scratch_shapes=[pltpu.VMEM((1, N_EXPERTS), F32)],
        compiler_params=_params(("arbitrary",)),
        name="outproj_route",
    )(x, pool_out, attn, w_o, g_ffn, wr_hi, wr_lo, router_bias, ws_gate, ws_up, ws_down, counts_in)


def _dispatch_kernel(dest_ref, hp_ref, *rest):
    xs_ref, sem = rest[-2:]
    tm = hp_ref.shape[0]

    def row_copy(r, d):
        return pltpu.make_async_copy(hp_ref.at[pl.ds(r, 1)], xs_ref.at[pl.ds(d, 1)], sem)

    def issue(r, carry):
        for k in range(TOP_K):
            row_copy(r, dest_ref[r * TOP_K + k]).start(priority=k % 2)
        return carry

    def drain(r, carry):
        for k in range(TOP_K):
            row_copy(r, dest_ref[r * TOP_K + k]).wait()
        return carry

    lax.fori_loop(0, tm, issue, 0)
    lax.fori_loop(0, tm, drain, 0)


def _dispatch(dest, hp, xs=None, n_rows=None):
    t = hp.shape[0]
    tm = min(TM_ROWS, t)
    assert t % tm == 0
    in_specs = [pl.BlockSpec((tm * TOP_K,), lambda i: (i,), memory_space=pltpu.SMEM),
                pl.BlockSpec((tm, HALF), lambda i: (i, 0))]
    args = [dest, hp]
    if xs is not None:
        in_specs.append(pl.BlockSpec(memory_space=pl.ANY))
        args.append(xs)
        n_rows = xs.shape[0]
    return pl.pallas_call(
        _dispatch_kernel,
        grid=(t // tm,),
        in_specs=in_specs,
        out_specs=pl.BlockSpec(memory_space=pl.ANY),
        out_shape=jax.ShapeDtypeStruct((n_rows, HALF), U32),
        scratch_shapes=[pltpu.SemaphoreType.DMA],
        input_output_aliases={2: 0} if xs is not None else {},
        compiler_params=_params(("arbitrary",)),
        name="moe_dispatch",
    )(*args)


def _unpack(p):
    lo = lax.bitcast_convert_type(p << 16, F32).astype(BF16)
    hi = lax.bitcast_convert_type(p & jnp.uint32(0xFFFF0000), F32).astype(BF16)
    return lo, hi


def _expert_kernel(blk_e_ref, nblk_ref, valid_ref, first_ref, slot_ref, next_ref,
                   xs_ref, wg_hbm, wu_hbm, wd_hbm, y_ref,
                   wg_f32, wu_f32, wd_f32, wg_sc, wu_sc, wd_sc, sems):
    b = pl.program_id(0)
    slot = slot_ref[b]

    def weight_copies(e, s):
        return (pltpu.make_async_copy(wg_hbm.at[e], wg_f32.at[s], sems.at[s, 0]),
                pltpu.make_async_copy(wu_hbm.at[e], wu_f32.at[s], sems.at[s, 1]),
                pltpu.make_async_copy(wd_hbm.at[e], wd_f32.at[s], sems.at[s, 2]))

    @pl.when(b == 0)
    def _():
        for cp in weight_copies(blk_e_ref[0], 0):
            cp.start()

    @pl.when(first_ref[b] == 1)
    def _():
        for cp in weight_copies(blk_e_ref[b], slot):
            cp.wait()

        @pl.when(next_ref[b] >= 0)
        def _():
            for cp in weight_copies(next_ref[b], 1 - slot):
                cp.start()

        wg_sc[...] = wg_f32[slot].astype(BF16)
        wu_sc[...] = wu_f32[slot].astype(BF16)
        wd_sc[...] = wd_f32[slot].astype(BF16)

    @pl.when(b < nblk_ref[0])
    def _():
        row = lax.broadcasted_iota(jnp.int32, (xs_ref.shape[0], 1), 0)
        lo, hi = _unpack(jnp.where(row < valid_ref[b], xs_ref[...], jnp.uint32(0)))
        g = _dot(lo, wg_sc[:HALF, :]) + _dot(hi, wg_sc[HALF:, :])
        up = _dot(lo, wu_sc[:HALF, :]) + _dot(hi, wu_sc[HALF:, :])
        act = (g * _sigmoid(g)) * up
        y_ref[...] = _pack_rows(_dot(act.astype(BF16), wd_sc[...]).astype(BF16))


def _expert_tables(counts, pends, pstarts, nb, m):
    experts = jnp.arange(N_EXPERTS, dtype=jnp.int32)
    blk_row = jnp.arange(nb, dtype=jnp.int32) * m
    nblk = pends[-1] // m
    blk_e = jnp.minimum(jnp.sum((pends[None, :] <= blk_row[:, None]).astype(jnp.int32), axis=1), N_EXPERTS - 1)
    of_blk = blk_e[:, None] == experts
    pick = lambda table: jnp.sum(jnp.where(of_blk, table, 0), axis=1)
    blk_valid = jnp.clip(pick(counts) - (blk_row - pick(pstarts)), 0, m)
    prev_e = jnp.concatenate([jnp.full((1,), -1, jnp.int32), blk_e[:-1]])
    first = ((blk_row < nblk * m) & (blk_e != prev_e)).astype(jnp.int32)
    slot = (jnp.cumsum(first) - 1) % 2
    later_used = lax.cummin(jnp.where(counts > 0, experts, N_EXPERTS), axis=0, reverse=True)
    next_used = jnp.concatenate([later_used[1:], jnp.full((1,), N_EXPERTS, jnp.int32)])
    next_used = jnp.where(next_used < N_EXPERTS, next_used, -1)
    i32 = lambda a: a.astype(jnp.int32)
    return i32(blk_e), i32(nblk.reshape(1)), i32(blk_valid), first, i32(jnp.maximum(slot, 0)), i32(pick(next_used))


def _experts(tables, xs, w_gate, w_up, w_down):
    nb = tables[0].shape[0]
    m = M_EXPERT
    last = lambda b, nblk_ref: jnp.minimum(b, nblk_ref[0] - 1)
    grid_spec = pltpu.PrefetchScalarGridSpec(
        num_scalar_prefetch=len(tables),
        grid=(nb,),
        in_specs=[pl.BlockSpec((m, HALF), lambda b, be, nr, *_: (last(b, nr), 0)),
                  pl.BlockSpec(memory_space=pl.ANY), pl.BlockSpec(memory_space=pl.ANY),
                  pl.BlockSpec(memory_space=pl.ANY)],
        out_specs=pl.BlockSpec((m, HALF), lambda b, be, nr, *_: (last(b, nr), 0)),
        scratch_shapes=[pltpu.VMEM((2, D_MODEL, D_EXPERT), F32), pltpu.VMEM((2, D_MODEL, D_EXPERT), F32),
                        pltpu.VMEM((2, D_EXPERT, D_MODEL), F32),
                        pltpu.VMEM((D_MODEL, D_EXPERT), BF16), pltpu.VMEM((D_MODEL, D_EXPERT), BF16),
                        pltpu.VMEM((D_EXPERT, D_MODEL), BF16),
                        pltpu.SemaphoreType.DMA((2, 3))],
    )
    return pl.pallas_call(
        _expert_kernel,
        grid_spec=grid_spec,
        out_shape=jax.ShapeDtypeStruct((nb * m, HALF), U32),
        compiler_params=_params(("arbitrary",)),
        name="moe_experts",
    )(*tables, xs, w_gate, w_up, w_down)


def _combine_kernel(dest_ref, dest_next_ref, base_ref, gate_ref, ys_ref, o_ref, buf, sems):
    tm = base_ref.shape[0]
    i = pl.program_id(0)
    slot = i % 2

    def row_copy(d_ref, s, r, k):
        return pltpu.make_async_copy(ys_ref.at[pl.ds(d_ref[r * TOP_K + k], 1)], buf.at[s, k, pl.ds(r, 1)],
                                     sems.at[s])

    def issue_all(d_ref, s):
        def issue(r, carry):
            for k in range(TOP_K):
                row_copy(d_ref, s, r, k).start(priority=k % 2)
            return carry
        lax.fori_loop(0, tm, issue, 0)

    @pl.when(i == 0)
    def _():
        issue_all(dest_ref, 0)

    @pl.when(i + 1 < pl.num_programs(0))
    def _():
        issue_all(dest_next_ref, 1 - slot)

    def drain(r, carry):
        for k in range(TOP_K):
            row_copy(dest_ref, slot, r, k).wait()
        return carry

    lax.fori_loop(0, tm, drain, 0)
    for r0 in range(0, tm, COMBINE_ROWS):
        rows = slice(r0, r0 + COMBINE_ROWS)
        acc_lo = base_ref[rows, :HALF]
        acc_hi = base_ref[rows, HALF:]
        for k in range(TOP_K):
            g = gate_ref[rows, k:k + 1]
            lo, hi = _unpack_rows_f32(buf[slot, k, rows, :])
            acc_lo = acc_lo + g * lo
            acc_hi = acc_hi + g * hi
        o_ref[rows, :HALF] = acc_lo
        o_ref[rows, HALF:] = acc_hi


def _combine(dest, base, gates, ys):
    t = base.shape[0]
    tm = min(TM_ROWS, t)
    assert t % tm == 0
    n = t // tm
    return pl.pallas_call(
        _combine_kernel,
        grid=(n,),
        in_specs=[pl.BlockSpec((tm * TOP_K,), lambda i: (i,), memory_space=pltpu.SMEM),
                  pl.BlockSpec((tm * TOP_K,), lambda i: (jnp.minimum(i + 1, n - 1),), memory_space=pltpu.SMEM),
                  pl.BlockSpec((tm, D_MODEL), lambda i: (i, 0)),
                  pl.BlockSpec((tm, TOP_K), lambda i: (i, 0)),
                  pl.BlockSpec(memory_space=pl.ANY)],
        out_specs=pl.BlockSpec((tm, D_MODEL), lambda i: (i, 0)),
        out_shape=jax.ShapeDtypeStruct((t, D_MODEL), F32),
        scratch_shapes=[pltpu.VMEM((2, TOP_K, tm, HALF), U32), pltpu.SemaphoreType.DMA((2,))],
        compiler_params=_params(("arbitrary",)),
        name="moe_combine",
    )(dest, dest, base, gates, ys)


def _mixer(x, prefix, start_pos, w, init_c):
    b, l, _ = x.shape
    u, q, k, v, logf, kb, vb = _inproj(x.reshape(b * l, D_MODEL), w["g_mix"], w["w_main"], w["w_f"], w["b_f"],
                                       w["g_q"], w["g_k"])
    u = u.reshape(b, l, POOL_WIDTH)
    pool_out = _pool(u, prefix, w["w_pool"], w["pool_scale"], start_pos)
    c = _cumsum(logf.reshape(b, l, N_HEADS), init_c)
    shp = (b, l, ATTN_WIDTH)
    return u, q.reshape(shp), k.reshape(shp), v.reshape(shp), logf, pool_out, c, kb.reshape(shp), vb.reshape(shp)


def _slot_starts(idx, pstarts):
    experts = jnp.arange(N_EXPERTS, dtype=jnp.int32)
    return jnp.sum(jnp.where(idx[..., None] == experts, pstarts, 0), axis=-1)


def kernel(x_prompt, x_sample, cache_k, cache_v, cache_logf, cache_pool, g_mix, w_in, b_f, g_q, g_k, w_pool, pool_scale, w_o, g_ffn, w_router, router_bias, w_gate, w_up, w_down, ws_gate, ws_up, ws_down):
    depth = w_in.shape[0]
    assert depth == 1, "single-layer step"
    bp, sp, _ = x_prompt.shape
    bs, ls, _ = x_sample.shape
    past = cache_k.shape[2]
    assert sp >= POOL_BUF and ls >= POOL_BUF
    n_main = POOL_WIDTH + 3 * ATTN_WIDTH

    w = dict(
        g_mix=g_mix[0][None], g_q=g_q[0][None], g_k=g_k[0][None], b_f=b_f[0][None],
        w_main=w_in[0, :, :n_main].astype(BF16),
        w_f=jnp.pad(w_in[0, :, n_main:], ((0, 0), (0, HEAD_DIM - N_HEADS))).astype(BF16),
        w_pool=w_pool[0].astype(BF16), pool_scale=pool_scale[0][None],
    )
    wr_hi = w_router[0].astype(BF16)
    wr_lo = (w_router[0] - wr_hi.astype(F32)).astype(BF16)
    fin = (w_o[0].astype(BF16), g_ffn[0][None], wr_hi, wr_lo, router_bias[0][None],
           ws_gate[0].astype(BF16), ws_up[0].astype(BF16), ws_down[0].astype(BF16))
    qk_bound = 1.02 * HEAD_DIM ** 0.5 * jnp.max(jnp.abs(g_q[0])) * jnp.max(jnp.abs(g_k[0]))

    zero_c = lambda b: jnp.zeros((b, 1, N_HEADS), F32)

    u_p, q_p, k_p, v_p, logf_p, pool_p, c_p, kb_p, vb_p = _mixer(
        x_prompt, jnp.zeros((bp, POOL_HALO, POOL_WIDTH), F32), 0, w, zero_c(bp))
    attn_p = _prompt_attention(q_p, kb_p, vb_p, c_p, qk_bound)

    ck = cache_k[0].reshape(bs, past * N_HEADS, HEAD_DIM)
    cv = cache_v[0].reshape(bs, past * N_HEADS, HEAD_DIM)
    c_cache = _cumsum(cache_logf[0], zero_c(bs))
    prefix_s = jnp.pad(cache_pool[0], ((0, 0), (POOL_HALO - POOL_BUF, 0), (0, 0)))
    u_s, q_s, k_s, v_s, logf_s, pool_s, c_s, _, _ = _mixer(x_sample, prefix_s, past, w,
                                                           c_cache[:, past - 1:past, :])
    attn_s = _sample_attention(q_s, ck, cv, k_s, v_s, c_s, c_cache.transpose(0, 2, 1), c_s.transpose(0, 2, 1))

    tp, ts = bp * sp, bs * ls
    base_p, hp_p, idx_p, rank_p, gate_p, cnt_p = _finish(
        x_prompt.reshape(tp, D_MODEL), pool_p.reshape(tp, POOL_WIDTH), attn_p.reshape(tp, ATTN_WIDTH), *fin,
        jnp.zeros((1, N_EXPERTS), F32))
    base_s, hp_s, idx_s, rank_s, gate_s, cnt_s = _finish(
        x_sample.reshape(ts, D_MODEL), pool_s.reshape(ts, POOL_WIDTH), attn_s.reshape(ts, ATTN_WIDTH), *fin, cnt_p)

    m = M_EXPERT
    counts = cnt_s[0].astype(jnp.int32)
    pcounts = (counts + m - 1) // m * m
    pends = jnp.cumsum(pcounts)
    pstarts = pends - pcounts
    nb = -(-(tp + ts) * TOP_K // m) + N_EXPERTS
    tables = _expert_tables(counts, pends, pstarts, nb, m)
    dest_p = (_slot_starts(idx_p, pstarts) + rank_p).reshape(tp * TOP_K)
    dest_s = (_slot_starts(idx_s, pstarts) + rank_s).reshape(ts * TOP_K)

    xs = _dispatch(dest_p, hp_p, n_rows=nb * m)
    xs = _dispatch(dest_s, hp_s, xs)
    ys = _experts(tables, xs, w_gate[0], w_up[0], w_down[0])
    y_p = _combine(dest_p, base_p, gate_p, ys).reshape(bp, sp, D_MODEL)
    y_s = _combine(dest_s, base_s, gate_s, ys).reshape(bs, ls, D_MODEL)

    heads = lambda a, b, l: a.reshape(1, b, l, N_HEADS, HEAD_DIM)
    return (y_p, y_s,
            heads(k_p, bp, sp), heads(v_p, bp, sp), logf_p.reshape(1, bp, sp, N_HEADS), u_p[None, :, sp - POOL_BUF:, :],
            heads(k_s, bs, ls), heads(v_s, bs, ls), logf_s.reshape(1, bs, ls, N_HEADS), u_s[None, :, ls - POOL_BUF:, :])
```

```python
import functools

import jax
import jax.numpy as jnp
from jax import lax
from jax.experimental import pallas as pl
from jax.experimental.pallas import tpu as pltpu

D_MODEL = 2048
POOL_WIDTH = 1024
ATTN_WIDTH = 1024
HEAD_DIM = 128
N_HEADS = 8
POOL_WINDOWS = (2, 4, 8, 16)
POOL_GROUP = 256
POOL_BUF = 15
POOL_HALO = 16
N_EXPERTS = 64
TOP_K = 8
D_EXPERT = 512
D_SHARED = 512
ROUTED_SCALE = 2.5
EPS = 1e-6
NEG = -1e30

F32 = jnp.float32
BF16 = jnp.bfloat16
U32 = jnp.uint32
HALF = D_MODEL // 2

V7X_VMEM_LIMIT = 56 * 1024 * 1024

TM_PROJ = 256
TM_FINISH = 256
TL_SCAN = 512
TL_POOL = 512
TQ_ATTN = 512
TK_ATTN = 256
EXP_UNDERFLOW = 110.0
STATIC_SOFTMAX_RANGE = 60.0
TK_CACHE = 1024
TM_ROWS = 128
M_EXPERT = 256
COMBINE_ROWS = 8


def _params(sem, vmem=V7X_VMEM_LIMIT):
    return pltpu.CompilerParams(dimension_semantics=sem, vmem_limit_bytes=vmem)


def _resident(shape):
    zeros = (0,) * len(shape)
    return pl.BlockSpec(shape, lambda *_: zeros, pipeline_mode=pl.Buffered(1))


def _dot(a, b):
    return jnp.dot(a, b, preferred_element_type=F32)


def _sigmoid(x):
    return 1.0 / (1.0 + jnp.exp(-x))


def _pack_rows(xb):
    bits = lax.bitcast_convert_type(xb.astype(F32), U32)
    return (bits[:, :HALF] >> 16) | (bits[:, HALF:] & jnp.uint32(0xFFFF0000))


def _unpack_rows_f32(p):
    return (lax.bitcast_convert_type(p << 16, F32),
            lax.bitcast_convert_type(p & jnp.uint32(0xFFFF0000), F32))


def _inproj_kernel(x_ref, gmix_ref, w_ref, wf_ref, bf_ref, gq_ref, gk_ref,
                   u_ref, q_ref, k_ref, v_ref, logf_ref, kb_ref, vb_ref):
    x = x_ref[...]
    ms = jnp.mean(x * x, axis=-1, keepdims=True)
    h = (x * lax.rsqrt(ms + EPS) * gmix_ref[...]).astype(BF16)
    p, a = POOL_WIDTH, ATTN_WIDTH
    u_ref[...] = _dot(h, w_ref[:, :p])
    zq = _dot(h, w_ref[:, p:p + a])
    zk = _dot(h, w_ref[:, p + a:p + 2 * a])
    v = _dot(h, w_ref[:, p + 2 * a:p + 3 * a])
    v_ref[...] = v
    vb_ref[...] = v.astype(BF16)
    scale = HEAD_DIM ** -0.5
    for hd in range(N_HEADS):
        sl = slice(hd * HEAD_DIM, (hd + 1) * HEAD_DIM)
        qh = zq[:, sl]
        qn = qh * lax.rsqrt(jnp.mean(qh * qh, axis=-1, keepdims=True) + EPS) * gq_ref[...]
        q_ref[:, sl] = (qn * scale).astype(BF16)
        kh = zk[:, sl]
        kn = kh * lax.rsqrt(jnp.mean(kh * kh, axis=-1, keepdims=True) + EPS) * gk_ref[...]
        k_ref[:, sl] = kn
        kb_ref[:, sl] = kn.astype(BF16)
    zf = _dot(h, wf_ref[...])[:, :N_HEADS] + bf_ref[...]
    logf_ref[...] = jnp.minimum(zf, 0.0) - jnp.log1p(jnp.exp(-jnp.abs(zf)))


def _inproj(x, g_mix, w_main, w_f, b_f, g_q, g_k):
    t = x.shape[0]
    tm = min(TM_PROJ, t)
    assert t % tm == 0
    row = lambda w: pl.BlockSpec((tm, w), lambda i: (i, 0))
    return pl.pallas_call(
        _inproj_kernel,
        grid=(t // tm,),
        in_specs=[row(D_MODEL), _resident((1, D_MODEL)), _resident(w_main.shape), _resident(w_f.shape),
                  _resident((1, N_HEADS)), _resident((1, HEAD_DIM)), _resident((1, HEAD_DIM))],
        out_specs=[row(POOL_WIDTH), row(ATTN_WIDTH), row(ATTN_WIDTH), row(ATTN_WIDTH), row(N_HEADS),
                   row(ATTN_WIDTH), row(ATTN_WIDTH)],
        out_shape=[jax.ShapeDtypeStruct((t, POOL_WIDTH), F32),
                   jax.ShapeDtypeStruct((t, ATTN_WIDTH), BF16),
                   jax.ShapeDtypeStruct((t, ATTN_WIDTH), F32),
                   jax.ShapeDtypeStruct((t, ATTN_WIDTH), F32),
                   jax.ShapeDtypeStruct((t, N_HEADS), F32),
                   jax.ShapeDtypeStruct((t, ATTN_WIDTH), BF16),
                   jax.ShapeDtypeStruct((t, ATTN_WIDTH), BF16)],
        compiler_params=_params(("arbitrary",)),
        name="inproj",
    )(x, g_mix, w_main, w_f, b_f, g_q, g_k)


def _cumsum_kernel(x_ref, init_ref, o_ref, carry_ref):
    @pl.when(pl.program_id(1) == 0)
    def _():
        carry_ref[...] = init_ref[0]

    x = x_ref[0]
    tl = x.shape[0]
    r = lax.broadcasted_iota(jnp.int32, (tl, tl), 0)
    c = lax.broadcasted_iota(jnp.int32, (tl, tl), 1)
    tri = (r >= c).astype(F32)
    cs = jnp.dot(tri, x, precision=lax.Precision.HIGHEST, preferred_element_type=F32) + carry_ref[...]
    o_ref[0] = cs
    carry_ref[...] = cs[tl - 1:tl, :]


def _cumsum(x, init):
    b, l, h = x.shape
    tl = min(TL_SCAN, l)
    assert l % tl == 0
    return pl.pallas_call(
        _cumsum_kernel,
        grid=(b, l // tl),
        in_specs=[pl.BlockSpec((1, tl, h), lambda bi, li: (bi, li, 0)),
                  pl.BlockSpec((1, 1, h), lambda bi, li: (bi, 0, 0))],
        out_specs=pl.BlockSpec((1, tl, h), lambda bi, li: (bi, li, 0)),
        out_shape=jax.ShapeDtypeStruct((b, l, h), F32),
        scratch_shapes=[pltpu.VMEM((1, h), F32)],
        compiler_params=_params(("arbitrary", "arbitrary")),
        name="logf_cumsum",
    )(x, init)


def _pool_kernel(u_ref, prefix_ref, wp_ref, ps_ref, o_ref, ext_ref, *, start_pos, tl):
    l = pl.program_id(1)

    @pl.when(l == 0)
    def _():
        ext_ref[0:POOL_HALO, :] = prefix_ref[0]

    u = u_ref[0]
    ext_ref[POOL_HALO:POOL_HALO + tl, :] = u
    pos = start_pos + l * tl + lax.broadcasted_iota(jnp.int32, (tl, 1), 0)
    for g, w in enumerate(POOL_WINDOWS):
        cols = slice(g * POOL_GROUP, (g + 1) * POOL_GROUP)
        s = u[:, cols]
        for j in range(1, w):
            s = s + ext_ref[POOL_HALO - j:POOL_HALO - j + tl, cols]
        cnt = jnp.minimum(w, pos + 1).astype(F32)
        d = s / cnt - u[:, cols]
        og = _dot(d.astype(BF16), wp_ref[g]) * ps_ref[:, cols]
        o_ref[0, :, cols] = og.astype(o_ref.dtype)
    ext_ref[0:POOL_HALO, :] = ext_ref[tl:tl + POOL_HALO, :]


def _pool(u, prefix, w_pool, pool_scale, start_pos):
    b, l, p = u.shape
    tl = min(TL_POOL, l)
    assert l % tl == 0 and tl >= POOL_HALO
    return pl.pallas_call(
        functools.partial(_pool_kernel, start_pos=start_pos, tl=tl),
        grid=(b, l // tl),
        in_specs=[pl.BlockSpec((1, tl, p), lambda bi, li: (bi, li, 0)),
                  pl.BlockSpec((1, POOL_HALO, p), lambda bi, li: (bi, 0, 0)),
                  _resident(w_pool.shape), _resident((1, p))],
        out_specs=pl.BlockSpec((1, tl, p), lambda bi, li: (bi, li, 0)),
        out_shape=jax.ShapeDtypeStruct((b, l, p), BF16),
        scratch_shapes=[pltpu.VMEM((tl + POOL_HALO, p), F32)],
        compiler_params=_params(("arbitrary", "arbitrary")),
        name="pool_mix",
    )(u, prefix, w_pool, pool_scale)


def _softmax_step(s, v, m_prev, l_prev, acc_prev):
    m_new = jnp.maximum(m_prev, jnp.max(s, axis=1, keepdims=True))
    alpha = jnp.exp(m_prev - m_new)
    p = jnp.exp(s - m_new)
    l_new = alpha * l_prev + jnp.sum(p, axis=1, keepdims=True)
    acc_new = alpha * acc_prev + _dot(p.astype(BF16), v)
    return m_new, l_new, acc_new


def _qk(q, k):
    return lax.dot_general(q, k, (((1,), (1,)), ((), ())), preferred_element_type=F32)


def _causal(s):
    r = lax.broadcasted_iota(jnp.int32, s.shape, 0)
    c = lax.broadcasted_iota(jnp.int32, s.shape, 1)
    return jnp.where(r >= c, s, NEG)


def _flash_kernel(tab_ref, q_ref, k_ref, v_ref, cq_ref, ck_ref, o_ref, m_sc, l_sc, acc_sc, cq_sc,
                  *, tq, tk, nq, nk, nbh):
    hd = pl.program_id(1)
    i = pl.program_id(2)
    bh = pl.program_id(0) * N_HEADS + hd
    r = tq // tk

    m_sc[...] = jnp.full(m_sc.shape, NEG, F32)
    l_sc[...] = jnp.zeros(l_sc.shape, F32)
    acc_sc[...] = jnp.zeros(acc_sc.shape, F32)
    cq = cq_ref[0]
    lane = lax.broadcasted_iota(jnp.int32, cq.shape, 1)
    cq_sc[...] = jnp.sum(jnp.where(lane == hd, cq, 0.0), axis=1, keepdims=True)
    q = q_ref[0]

    def tile(jj, masked):
        off = pl.multiple_of(jj * tk, tk)
        s = _qk(q, k_ref[0, pl.ds(off, tk), :]) + cq_sc[...] - ck_ref[0, pl.ds(jj, 1), :]
        if masked:
            rr = lax.broadcasted_iota(jnp.int32, s.shape, 0) + i * tq
            cc = lax.broadcasted_iota(jnp.int32, s.shape, 1) + off
            s = jnp.where(rr >= cc, s, NEG)
        m, l, acc = _softmax_step(s, v_ref[0, pl.ds(off, tk), :], m_sc[...], l_sc[...], acc_sc[...])
        m_sc[...] = m
        l_sc[...] = l
        acc_sc[...] = acc

    for d in range(r):
        tile(i * r + d, True)

    bound = tab_ref[0]
    cmax_i = tab_ref[2 + bh * nq + i]
    cmin_base = 2 + nbh * nq + bh * nk

    def live(jj):
        cmin_j = tab_ref[cmin_base + jnp.maximum(jj, 0)]
        return (jj >= 0) & (bound + cmax_i - cmin_j >= -EXP_UNDERFLOW)

    def body(jj):
        tile(jj, False)
        return jj - 1

    lax.while_loop(live, body, i * r - 1)
    o_ref[0] = (acc_sc[...] / l_sc[...]).astype(o_ref.dtype)


def _flash_static_kernel(tab_ref, q_ref, k_ref, v_ref, cq_ref, ck_ref, o_ref, vaug_sc, acc_sc, cqb_sc,
                         *, tq, tk, nq, nk, nbh):
    hd = pl.program_id(1)
    i = pl.program_id(2)
    bh = pl.program_id(0) * N_HEADS + hd
    r = tq // tk

    @pl.when(i == 0)
    def _():
        vaug_sc[:, :HEAD_DIM] = v_ref[0]
        vaug_sc[:, HEAD_DIM:] = jnp.ones((vaug_sc.shape[0], HEAD_DIM), BF16)

    cq = cq_ref[0]
    lane = lax.broadcasted_iota(jnp.int32, cq.shape, 1)
    col = jnp.sum(jnp.where(lane == hd, cq, 0.0), axis=1, keepdims=True) - tab_ref[1]
    cqb_sc[...] = jnp.broadcast_to(col, cqb_sc.shape)
    acc_sc[...] = jnp.zeros(acc_sc.shape, F32)
    q = q_ref[0]

    def tile(jj, masked):
        off = pl.multiple_of(jj * tk, tk)
        s = _qk(q, k_ref[0, pl.ds(off, tk), :])
        ck = ck_ref[0, pl.ds(jj, 1), :]
        parts = []
        for c in range(tk // HEAD_DIM):
            sl = slice(c * HEAD_DIM, (c + 1) * HEAD_DIM)
            sc = s[:, sl] + cqb_sc[...] - ck[:, sl]
            if masked:
                rr = lax.broadcasted_iota(jnp.int32, sc.shape, 0) + i * tq
                cc = lax.broadcasted_iota(jnp.int32, sc.shape, 1) + (off + c * HEAD_DIM)
                sc = jnp.where(rr >= cc, sc, NEG)
            parts.append(jnp.exp(sc).astype(BF16))
        acc_sc[...] += _dot(jnp.concatenate(parts, axis=1), vaug_sc[pl.ds(off, tk), :])

    for d in range(r):
        tile(i * r + d, True)

    cmax_i = tab_ref[2 + bh * nq + i]
    cmin_base = 2 + nbh * nq + bh * nk

    def live(jj):
        cmin_j = tab_ref[cmin_base + jnp.maximum(jj, 0)]
        return (jj >= 0) & (tab_ref[0] + cmax_i - cmin_j >= -EXP_UNDERFLOW)

    def body(jj):
        tile(jj, False)
        return jj - 1

    lax.while_loop(live, body, i * r - 1)
    o_ref[0] = (acc_sc[:, :HEAD_DIM] / acc_sc[:, HEAD_DIM:]).astype(o_ref.dtype)


def _skip_table(c, tq, tk, slack, shift):
    b, s, h = c.shape
    cmax = c.reshape(b, s // tq, tq, h).max(axis=2)
    cmin = lax.cummin(c.reshape(b, s // tk, tk, h).min(axis=2), axis=1)
    flat = lambda a: a.transpose(0, 2, 1).reshape(-1)
    return jnp.concatenate([slack.reshape(1), shift.reshape(1), flat(cmax), flat(cmin)]).astype(F32)


def _prompt_attention(q, k, v, c_col, qk_bound):
    b, s, _ = q.shape
    tq = min(TQ_ATTN, s)
    tk = min(TK_ATTN, tq)
    assert s % tq == 0 and tq % tk == 0 and tk % HEAD_DIM == 0
    nq, nk = s // tq, s // tk
    c_row = c_col.transpose(0, 2, 1).reshape(b * N_HEADS, nk, tk)
    kv_spec = pl.BlockSpec((1, s, HEAD_DIM), lambda bi, hi, i, tab: (bi, 0, hi))
    in_specs = [pl.BlockSpec((1, tq, HEAD_DIM), lambda bi, hi, i, tab: (bi, i, hi)),
                kv_spec, kv_spec,
                pl.BlockSpec((1, tq, N_HEADS), lambda bi, hi, i, tab: (bi, i, 0)),
                pl.BlockSpec((1, nk, tk), lambda bi, hi, i, tab: (bi * N_HEADS + hi, 0, 0))]
    out_spec = pl.BlockSpec((1, tq, HEAD_DIM), lambda bi, hi, i, tab: (bi, i, hi))
    static = dict(tq=tq, tk=tk, nq=nq, nk=nk, nbh=b * N_HEADS)

    def call(body, scratch, tab):
        return pl.pallas_call(
            functools.partial(body, **static),
            grid_spec=pltpu.PrefetchScalarGridSpec(num_scalar_prefetch=1, grid=(b, N_HEADS, nq), in_specs=in_specs,
                                                   out_specs=out_spec, scratch_shapes=scratch),
            out_shape=jax.ShapeDtypeStruct((b, s, ATTN_WIDTH), BF16),
            compiler_params=_params(("parallel", "parallel", "arbitrary")),
            name="prompt_attention",
        )(tab, q, k, v, c_col, c_row)

    def fixed_shift():
        tab = _skip_table(c_col, tq, tk, jnp.zeros((), F32), qk_bound + 1.0)
        return call(_flash_static_kernel,
                    [pltpu.VMEM((s, 2 * HEAD_DIM), BF16), pltpu.VMEM((tq, 2 * HEAD_DIM), F32),
                     pltpu.VMEM((tq, HEAD_DIM), F32)], tab)

    def running_max():
        tab = _skip_table(c_col, tq, tk, 2.0 * qk_bound + 2.0, jnp.zeros((), F32))
        return call(_flash_kernel,
                    [pltpu.VMEM((tq, 1), F32), pltpu.VMEM((tq, 1), F32), pltpu.VMEM((tq, HEAD_DIM), F32),
                     pltpu.VMEM((tq, 1), F32)], tab)

    return lax.cond(2.0 * qk_bound + 1.0 <= STATIC_SOFTMAX_RANGE, fixed_shift, running_max)


def _sample_attn_kernel(q_ref, ck_ref, cv_ref, kn_ref, vn_ref, cq_ref, cc_ref, cn_ref, o_ref,
                        m_sc, l_sc, acc_sc, *, nk, tk):
    j = pl.program_id(1)

    @pl.when(j == 0)
    def _():
        m_sc[...] = jnp.full(m_sc.shape, NEG, F32)
        l_sc[...] = jnp.zeros(l_sc.shape, F32)
        acc_sc[...] = jnp.zeros(acc_sc.shape, F32)

    def step(head_rows, k_ref, v_ref, crow_ref, masked):
        for hd in range(N_HEADS):
            sl = slice(hd * HEAD_DIM, (hd + 1) * HEAD_DIM)
            s = _qk(q_ref[0, :, sl], head_rows(k_ref, hd).astype(BF16))
            s = s + cq_ref[0, :, hd:hd + 1] - crow_ref[0, hd:hd + 1, :]
            if masked:
                s = _causal(s)
            m, l, acc = _softmax_step(s, head_rows(v_ref, hd).astype(BF16), m_sc[hd], l_sc[hd], acc_sc[hd])
            m_sc[hd] = m
            l_sc[hd] = l
            acc_sc[hd] = acc

    cached = lambda ref, hd: ref[0, pl.ds(hd, tk, stride=N_HEADS), :]
    fresh = lambda ref, hd: ref[0, :, hd * HEAD_DIM:(hd + 1) * HEAD_DIM]

    @pl.when(j < nk)
    def _():
        step(cached, ck_ref, cv_ref, cc_ref, False)

    @pl.when(j == nk)
    def _():
        step(fresh, kn_ref, vn_ref, cn_ref, True)
        for hd in range(N_HEADS):
            sl = slice(hd * HEAD_DIM, (hd + 1) * HEAD_DIM)
            o_ref[0, :, sl] = (acc_sc[hd] / l_sc[hd]).astype(o_ref.dtype)


def _sample_attention(q, cache_k, cache_v, k_new, v_new, cq_col, cc_row, cn_row):
    b, l, _ = q.shape
    past = cache_k.shape[1] // N_HEADS
    tk = min(TK_CACHE, past)
    assert past % tk == 0
    nk = past // tk
    cache_spec = pl.BlockSpec((1, tk * N_HEADS, HEAD_DIM), lambda bi, j: (bi, jnp.minimum(j, nk - 1), 0))
    new_spec = pl.BlockSpec((1, l, ATTN_WIDTH), lambda bi, j: (bi, 0, 0))
    return pl.pallas_call(
        functools.partial(_sample_attn_kernel, nk=nk, tk=tk),
        grid=(b, nk + 1),
        in_specs=[new_spec, cache_spec, cache_spec, new_spec, new_spec,
                  pl.BlockSpec((1, l, N_HEADS), lambda bi, j: (bi, 0, 0)),
                  pl.BlockSpec((1, N_HEADS, tk), lambda bi, j: (bi, 0, jnp.minimum(j, nk - 1))),
                  pl.BlockSpec((1, N_HEADS, l), lambda bi, j: (bi, 0, 0))],
        out_specs=new_spec,
        out_shape=jax.ShapeDtypeStruct((b, l, ATTN_WIDTH), BF16),
        scratch_shapes=[pltpu.VMEM((N_HEADS, l, 1), F32), pltpu.VMEM((N_HEADS, l, 1), F32),
                        pltpu.VMEM((N_HEADS, l, HEAD_DIM), F32)],
        compiler_params=_params(("parallel", "arbitrary")),
        name="sample_attention",
    )(q, cache_k, cache_v, k_new, v_new, cq_col, cc_row, cn_row)


def _finish_kernel(x_ref, po_ref, at_ref, wo_ref, gffn_ref, wrh_ref, wrl_ref, rb_ref, wsg_ref, wsu_ref, wsd_ref,
                   cin_ref,
                   base_ref, hp_ref, idx_ref, rank_ref, gate_ref, cnt_ref, carry_sc):
    @pl.when(pl.program_id(0) == 0)
    def _():
        carry_sc[...] = cin_ref[...]

    tm = x_ref.shape[0]
    x1 = x_ref[...] + _dot(po_ref[...], wo_ref[:POOL_WIDTH, :]) + _dot(at_ref[...], wo_ref[POOL_WIDTH:, :])
    ms = jnp.mean(x1 * x1, axis=-1, keepdims=True)
    h = x1 * lax.rsqrt(ms + EPS) * gffn_ref[...]
    hb = h.astype(BF16)

    g = _dot(hb, wsg_ref[...])
    up = _dot(hb, wsu_ref[...])
    act = (g * _sigmoid(g)) * up
    base_ref[...] = x1 + _dot(act.astype(BF16), wsd_ref[...])

    hp_ref[...] = _pack_rows(hb)

    h_lo = (h - hb.astype(F32)).astype(BF16)
    logits = _dot(hb, wrh_ref[...]) + (_dot(h_lo, wrh_ref[...]) + _dot(hb, wrl_ref[...]))
    scores = _sigmoid(logits)
    sel = scores + rb_ref[...]
    lane = lax.broadcasted_iota(jnp.int32, (tm, N_EXPERTS), 1).astype(F32)
    mask = jnp.zeros((tm, N_EXPERTS), F32)
    idx_cols, gate_cols = [], []
    for _ in range(TOP_K):
        mx = jnp.max(sel, axis=1, keepdims=True)
        ik = jnp.min(jnp.where(sel == mx, lane, float(N_EXPERTS)), axis=1, keepdims=True)
        oh = lane == ik
        gate_cols.append(jnp.sum(jnp.where(oh, scores, 0.0), axis=1, keepdims=True))
        idx_cols.append(ik)
        sel = jnp.where(oh, -jnp.inf, sel)
        mask = jnp.where(oh, 1.0, mask)

    r = lax.broadcasted_iota(jnp.int32, (tm, tm), 0)
    c = lax.broadcasted_iota(jnp.int32, (tm, tm), 1)
    tri = (r > c).astype(BF16)
    rank = _dot(tri, mask.astype(BF16)) + carry_sc[...]
    carry_sc[...] = carry_sc[...] + jnp.sum(mask, axis=0, keepdims=True)
    cnt_ref[...] = carry_sc[...]

    gsum = gate_cols[0]
    for gk in gate_cols[1:]:
        gsum = gsum + gk
    lane_k = lax.broadcasted_iota(jnp.int32, (tm, TOP_K), 1)
    idx_out = jnp.zeros((tm, TOP_K), F32)
    rank_out = jnp.zeros((tm, TOP_K), F32)
    gate_out = jnp.zeros((tm, TOP_K), F32)
    for k in range(TOP_K):
        rk = jnp.sum(jnp.where(lane == idx_cols[k], rank, 0.0), axis=1, keepdims=True)
        idx_out = jnp.where(lane_k == k, idx_cols[k], idx_out)
        rank_out = jnp.where(lane_k == k, rk, rank_out)
        gate_out = jnp.where(lane_k == k, gate_cols[k] / gsum * ROUTED_SCALE, gate_out)
    idx_ref[...] = idx_out.astype(jnp.int32)
    rank_ref[...] = rank_out.astype(jnp.int32)
    gate_ref[...] = gate_out


def _finish(x, pool_out, attn, w_o, g_ffn, wr_hi, wr_lo, router_bias, ws_gate, ws_up, ws_down, counts_in):
    t = x.shape[0]
    tm = min(TM_FINISH, t)
    assert t % tm == 0
    row = lambda w: pl.BlockSpec((tm, w), lambda i: (i, 0))
    return pl.pallas_call(
        _finish_kernel,
        grid=(t // tm,),
        in_specs=[row(D_MODEL), row(POOL_WIDTH), row(ATTN_WIDTH), _resident(w_o.shape), _resident((1, D_MODEL)),
                  _resident(wr_hi.shape), _resident(wr_lo.shape), _resident((1, N_EXPERTS)),
                  _resident(ws_gate.shape), _resident(ws_up.shape), _resident(ws_down.shape),
                  _resident((1, N_EXPERTS))],
        out_specs=[row(D_MODEL), row(HALF), row(TOP_K), row(TOP_K), row(TOP_K),
                   pl.BlockSpec((1, N_EXPERTS), lambda i: (0, 0))],
        out_shape=[jax.ShapeDtypeStruct((t, D_MODEL), F32),
                   jax.ShapeDtypeStruct((t, HALF), U32),
                   jax.ShapeDtypeStruct((t, TOP_K), jnp.int32),
                   jax.ShapeDtypeStruct((t, TOP_K), jnp.int32),
                   jax.ShapeDtypeStruct((t, TOP_K), F32),
                   jax.ShapeDtypeStruct((1, N_EXPERTS), F32)],
        scratch_shapes=[pltpu.VMEM((1, N_EXPERTS), F32)],
        compiler_params=_params(("arbitrary",)),
        name="outproj_route",
    )(x, pool_out, attn, w_o, g_ffn, wr_hi, wr_lo, router_bias, ws_gate, ws_up, ws_down, counts_in)


def _dispatch_kernel(dest_ref, hp_ref, *rest):
    xs_ref, sem = rest[-2:]
    tm = hp_ref.shape[0]

    def row_copy(r, d):
        return pltpu.make_async_copy(hp_ref.at[pl.ds(r, 1)], xs_ref.at[pl.ds(d, 1)], sem)

    def drain(r, carry):
        for k in range(TOP_K):
            row_copy(r, dest_ref[r * TOP_K + k]).wait()
        return carry

    for r in range(tm):
        for k in range(TOP_K):
            row_copy(r, dest_ref[r * TOP_K + k]).start(priority=k % 2)
    lax.fori_loop(0, tm, drain, 0)


def _dispatch(dest, hp, xs=None, n_rows=None):
    t = hp.shape[0]
    tm = min(TM_ROWS, t)
    assert t % tm == 0
    in_specs = [pl.BlockSpec((tm * TOP_K,), lambda i: (i,), memory_space=pltpu.SMEM),
                pl.BlockSpec((tm, HALF), lambda i: (i, 0))]
    args = [dest, hp]
    if xs is not None:
        in_specs.append(pl.BlockSpec(memory_space=pl.ANY))
        args.append(xs)
        n_rows = xs.shape[0]
    return pl.pallas_call(
        _dispatch_kernel,
        grid=(t // tm,),
        in_specs=in_specs,
        out_specs=pl.BlockSpec(memory_space=pl.ANY),
        out_shape=jax.ShapeDtypeStruct((n_rows, HALF), U32),
        scratch_shapes=[pltpu.SemaphoreType.DMA],
        input_output_aliases={2: 0} if xs is not None else {},
        compiler_params=_params(("arbitrary",)),
        name="moe_dispatch",
    )(*args)


def _unpack(p):
    lo = lax.bitcast_convert_type(p << 16, F32).astype(BF16)
    hi = lax.bitcast_convert_type(p & jnp.uint32(0xFFFF0000), F32).astype(BF16)
    return lo, hi


def _expert_kernel(blk_e_ref, nblk_ref, valid_ref, first_ref, slot_ref, next_ref,
                   xs_ref, wg_hbm, wu_hbm, wd_hbm, y_ref,
                   wg_f32, wu_f32, wd_f32, wg_sc, wu_sc, wd_sc, sems):
    b = pl.program_id(0)
    slot = slot_ref[b]

    def weight_copies(e, s):
        return (pltpu.make_async_copy(wg_hbm.at[e], wg_f32.at[s], sems.at[s, 0]),
                pltpu.make_async_copy(wu_hbm.at[e], wu_f32.at[s], sems.at[s, 1]),
                pltpu.make_async_copy(wd_hbm.at[e], wd_f32.at[s], sems.at[s, 2]))

    @pl.when(b == 0)
    def _():
        for cp in weight_copies(blk_e_ref[0], 0):
            cp.start()

    @pl.when(first_ref[b] == 1)
    def _():
        for cp in weight_copies(blk_e_ref[b], slot):
            cp.wait()

        @pl.when(next_ref[b] >= 0)
        def _():
            for cp in weight_copies(next_ref[b], 1 - slot):
                cp.start()

        wg_sc[...] = wg_f32[slot].astype(BF16)
        wu_sc[...] = wu_f32[slot].astype(BF16)
        wd_sc[...] = wd_f32[slot].astype(BF16)

    @pl.when(b < nblk_ref[0])
    def _():
        row = lax.broadcasted_iota(jnp.int32, (xs_ref.shape[0], 1), 0)
        lo, hi = _unpack(jnp.where(row < valid_ref[b], xs_ref[...], jnp.uint32(0)))
        g = _dot(lo, wg_sc[:HALF, :]) + _dot(hi, wg_sc[HALF:, :])
        up = _dot(lo, wu_sc[:HALF, :]) + _dot(hi, wu_sc[HALF:, :])
        act = (g * _sigmoid(g)) * up
        y_ref[...] = _pack_rows(_dot(act.astype(BF16), wd_sc[...]).astype(BF16))


def _expert_tables(counts, pends, pstarts, nb, m):
    experts = jnp.arange(N_EXPERTS, dtype=jnp.int32)
    blk_row = jnp.arange(nb, dtype=jnp.int32) * m
    nblk = pends[-1] // m
    blk_e = jnp.minimum(jnp.sum((pends[None, :] <= blk_row[:, None]).astype(jnp.int32), axis=1), N_EXPERTS - 1)
    of_blk = blk_e[:, None] == experts
    pick = lambda table: jnp.sum(jnp.where(of_blk, table, 0), axis=1)
    blk_valid = jnp.clip(pick(counts) - (blk_row - pick(pstarts)), 0, m)
    prev_e = jnp.concatenate([jnp.full((1,), -1, jnp.int32), blk_e[:-1]])
    first = ((blk_row < nblk * m) & (blk_e != prev_e)).astype(jnp.int32)
    slot = (jnp.cumsum(first) - 1) % 2
    later_used = lax.cummin(jnp.where(counts > 0, experts, N_EXPERTS), axis=0, reverse=True)
    next_used = jnp.concatenate([later_used[1:], jnp.full((1,), N_EXPERTS, jnp.int32)])
    next_used = jnp.where(next_used < N_EXPERTS, next_used, -1)
    i32 = lambda a: a.astype(jnp.int32)
    return i32(blk_e), i32(nblk.reshape(1)), i32(blk_valid), first, i32(jnp.maximum(slot, 0)), i32(pick(next_used))


def _experts(tables, xs, w_gate, w_up, w_down):
    nb = tables[0].shape[0]
    m = M_EXPERT
    last = lambda b, nblk_ref: jnp.minimum(b, nblk_ref[0] - 1)
    grid_spec = pltpu.PrefetchScalarGridSpec(
        num_scalar_prefetch=len(tables),
        grid=(nb,),
        in_specs=[pl.BlockSpec((m, HALF), lambda b, be, nr, *_: (last(b, nr), 0)),
                  pl.BlockSpec(memory_space=pl.ANY), pl.BlockSpec(memory_space=pl.ANY),
                  pl.BlockSpec(memory_space=pl.ANY)],
        out_specs=pl.BlockSpec((m, HALF), lambda b, be, nr, *_: (last(b, nr), 0)),
        scratch_shapes=[pltpu.VMEM((2, D_MODEL, D_EXPERT), F32), pltpu.VMEM((2, D_MODEL, D_EXPERT), F32),
                        pltpu.VMEM((2, D_EXPERT, D_MODEL), F32),
                        pltpu.VMEM((D_MODEL, D_EXPERT), BF16), pltpu.VMEM((D_MODEL, D_EXPERT), BF16),
                        pltpu.VMEM((D_EXPERT, D_MODEL), BF16),
                        pltpu.SemaphoreType.DMA((2, 3))],
    )
    return pl.pallas_call(
        _expert_kernel,
        grid_spec=grid_spec,
        out_shape=jax.ShapeDtypeStruct((nb * m, HALF), U32),
        compiler_params=_params(("arbitrary",)),
        name="moe_experts",
    )(*tables, xs, w_gate, w_up, w_down)


def _combine_kernel(dest_ref, dest_next_ref, base_ref, gate_ref, ys_ref, o_ref, buf, sems):
    tm = base_ref.shape[0]
    i = pl.program_id(0)

    def row_copy(d_ref, s, r, k):
        return pltpu.make_async_copy(ys_ref.at[pl.ds(d_ref[r * TOP_K + k], 1)], buf.at[s, k, pl.ds(r, 1)],
                                     sems.at[s])

    def issue_all(d_ref, s):
        for r in range(tm):
            for k in range(TOP_K):
                row_copy(d_ref, s, r, k).start(priority=k % 2)

    def step(s):
        @pl.when(i + 1 < pl.num_programs(0))
        def _():
            issue_all(dest_next_ref, 1 - s)

        def drain(r, carry):
            for k in range(TOP_K):
                row_copy(dest_ref, s, r, k).wait()
            return carry

        lax.fori_loop(0, tm, drain, 0)
        for r0 in range(0, tm, COMBINE_ROWS):
            rows = slice(r0, r0 + COMBINE_ROWS)
            acc_lo = base_ref[rows, :HALF]
            acc_hi = base_ref[rows, HALF:]
            for k in range(TOP_K):
                g = gate_ref[rows, k:k + 1]
                lo, hi = _unpack_rows_f32(buf[s, k, rows, :])
                acc_lo = acc_lo + g * lo
                acc_hi = acc_hi + g * hi
            o_ref[rows, :HALF] = acc_lo
            o_ref[rows, HALF:] = acc_hi

    @pl.when(i == 0)
    def _():
        issue_all(dest_ref, 0)

    for s in range(2):
        pl.when(i % 2 == s)(functools.partial(step, s))


def _combine(dest, base, gates, ys):
    t = base.shape[0]
    tm = min(TM_ROWS, t)
    assert t % tm == 0
    n = t // tm
    return pl.pallas_call(
        _combine_kernel,
        grid=(n,),
        in_specs=[pl.BlockSpec((tm * TOP_K,), lambda i: (i,), memory_space=pltpu.SMEM),
                  pl.BlockSpec((tm * TOP_K,), lambda i: (jnp.minimum(i + 1, n - 1),), memory_space=pltpu.SMEM),
                  pl.BlockSpec((tm, D_MODEL), lambda i: (i, 0)),
                  pl.BlockSpec((tm, TOP_K), lambda i: (i, 0)),
                  pl.BlockSpec(memory_space=pl.ANY)],
        out_specs=pl.BlockSpec((tm, D_MODEL), lambda i: (i, 0)),
        out_shape=jax.ShapeDtypeStruct((t, D_MODEL), F32),
        scratch_shapes=[pltpu.VMEM((2, TOP_K, tm, HALF), U32), pltpu.SemaphoreType.DMA((2,))],
        compiler_params=_params(("arbitrary",)),
        name="moe_combine",
    )(dest, dest, base, gates, ys)


def _mixer(x, prefix, start_pos, w, init_c):
    b, l, _ = x.shape
    u, q, k, v, logf, kb, vb = _inproj(x.reshape(b * l, D_MODEL), w["g_mix"], w["w_main"], w["w_f"], w["b_f"],
                                       w["g_q"], w["g_k"])
    u = u.reshape(b, l, POOL_WIDTH)
    pool_out = _pool(u, prefix, w["w_pool"], w["pool_scale"], start_pos)
    c = _cumsum(logf.reshape(b, l, N_HEADS), init_c)
    shp = (b, l, ATTN_WIDTH)
    return u, q.reshape(shp), k.reshape(shp), v.reshape(shp), logf, pool_out, c, kb.reshape(shp), vb.reshape(shp)


def _slot_starts(idx, pstarts):
    experts = jnp.arange(N_EXPERTS, dtype=jnp.int32)
    return jnp.sum(jnp.where(idx[..., None] == experts, pstarts, 0), axis=-1)


def kernel(x_prompt, x_sample, cache_k, cache_v, cache_logf, cache_pool, g_mix, w_in, b_f, g_q, g_k, w_pool, pool_scale, w_o, g_ffn, w_router, router_bias, w_gate, w_up, w_down, ws_gate, ws_up, ws_down):
    depth = w_in.shape[0]
    assert depth == 1, "single-layer step"
    bp, sp, _ = x_prompt.shape
    bs, ls, _ = x_sample.shape
    past = cache_k.shape[2]
    assert sp >= POOL_BUF and ls >= POOL_BUF
    n_main = POOL_WIDTH + 3 * ATTN_WIDTH

    w = dict(
        g_mix=g_mix[0][None], g_q=g_q[0][None], g_k=g_k[0][None], b_f=b_f[0][None],
        w_main=w_in[0, :, :n_main].astype(BF16),
        w_f=jnp.pad(w_in[0, :, n_main:], ((0, 0), (0, HEAD_DIM - N_HEADS))).astype(BF16),
        w_pool=w_pool[0].astype(BF16), pool_scale=pool_scale[0][None],
    )
    wr_hi = w_router[0].astype(BF16)
    wr_lo = (w_router[0] - wr_hi.astype(F32)).astype(BF16)
    fin = (w_o[0].astype(BF16), g_ffn[0][None], wr_hi, wr_lo, router_bias[0][None],
           ws_gate[0].astype(BF16), ws_up[0].astype(BF16), ws_down[0].astype(BF16))
    qk_bound = 1.02 * HEAD_DIM ** 0.5 * jnp.max(jnp.abs(g_q[0])) * jnp.max(jnp.abs(g_k[0]))

    zero_c = lambda b: jnp.zeros((b, 1, N_HEADS), F32)

    u_p, q_p, k_p, v_p, logf_p, pool_p, c_p, kb_p, vb_p = _mixer(
        x_prompt, jnp.zeros((bp, POOL_HALO, POOL_WIDTH), F32), 0, w, zero_c(bp))
    attn_p = _prompt_attention(q_p, kb_p, vb_p, c_p, qk_bound)

    ck = cache_k[0].reshape(bs, past * N_HEADS, HEAD_DIM)
    cv = cache_v[0].reshape(bs, past * N_HEADS, HEAD_DIM)
    c_cache = _cumsum(cache_logf[0], zero_c(bs))
    prefix_s = jnp.pad(cache_pool[0], ((0, 0), (POOL_HALO - POOL_BUF, 0), (0, 0)))
    u_s, q_s, k_s, v_s, logf_s, pool_s, c_s, _, _ = _mixer(x_sample, prefix_s, past, w,
                                                           c_cache[:, past - 1:past, :])
    attn_s = _sample_attention(q_s, ck, cv, k_s, v_s, c_s, c_cache.transpose(0, 2, 1), c_s.transpose(0, 2, 1))

    tp, ts = bp * sp, bs * ls
    base_p, hp_p, idx_p, rank_p, gate_p, cnt_p = _finish(
        x_prompt.reshape(tp, D_MODEL), pool_p.reshape(tp, POOL_WIDTH), attn_p.reshape(tp, ATTN_WIDTH), *fin,
        jnp.zeros((1, N_EXPERTS), F32))
    base_s, hp_s, idx_s, rank_s, gate_s, cnt_s = _finish(
        x_sample.reshape(ts, D_MODEL), pool_s.reshape(ts, POOL_WIDTH), attn_s.reshape(ts, ATTN_WIDTH), *fin, cnt_p)

    m = M_EXPERT
    counts = cnt_s[0].astype(jnp.int32)
    pcounts = (counts + m - 1) // m * m
    pends = jnp.cumsum(pcounts)
    pstarts = pends - pcounts
    nb = -(-(tp + ts) * TOP_K // m) + N_EXPERTS
    tables = _expert_tables(counts, pends, pstarts, nb, m)
    dest_p = (_slot_starts(idx_p, pstarts) + rank_p).reshape(tp * TOP_K)
    dest_s = (_slot_starts(idx_s, pstarts) + rank_s).reshape(ts * TOP_K)

    xs = _dispatch(dest_p, hp_p, n_rows=nb * m)
    xs = _dispatch(dest_s, hp_s, xs)
    ys = _experts(tables, xs, w_gate[0], w_up[0], w_down[0])
    y_p = _combine(dest_p, base_p, gate_p, ys).reshape(bp, sp, D_MODEL)
    y_s = _combine(dest_s, base_s, gate_s, ys).reshape(bs, ls, D_MODEL)

    heads = lambda a, b, l: a.reshape(1, b, l, N_HEADS, HEAD_DIM)
    return (y_p, y_s,
            heads(k_p, bp, sp), heads(v_p, bp, sp), logf_p.reshape(1, bp, sp, N_HEADS), u_p[None, :, sp - POOL_BUF:, :],
            heads(k_s, bs, ls), heads(v_s, bs, ls), logf_s.reshape(1, bs, ls, N_HEADS), u_s[None, :, ls - POOL_BUF:, :])
```

```python
import functools

import jax
import jax.numpy as jnp
from jax import lax
from jax.experimental import pallas as pl
from jax.experimental.pallas import tpu as pltpu

D_MODEL = 2048
POOL_WIDTH = 1024
ATTN_WIDTH = 1024
HEAD_DIM = 128
N_HEADS = 8
POOL_WINDOWS = (2, 4, 8, 16)
POOL_GROUP = 256
POOL_BUF = 15
POOL_HALO = 16
N_EXPERTS = 64
TOP_K = 8
D_EXPERT = 512
D_SHARED = 512
ROUTED_SCALE = 2.5
EPS = 1e-6
NEG = -1e30

F32 = jnp.float32
BF16 = jnp.bfloat16
U32 = jnp.uint32
HALF = D_MODEL // 2

V7X_VMEM_LIMIT = 56 * 1024 * 1024

TM_PROJ = 256
TM_FINISH = 256
TL_SCAN = 512
TL_POOL = 512
TQ_ATTN = 512
TK_ATTN = 256
EXP_UNDERFLOW = 110.0
STATIC_SOFTMAX_RANGE = 60.0
TK_CACHE = 1024
TM_ROWS = 128
M_EXPERT = 256
COMBINE_ROWS = 8


def _params(sem, vmem=V7X_VMEM_LIMIT):
    return pltpu.CompilerParams(dimension_semantics=sem, vmem_limit_bytes=vmem)


def _resident(shape):
    zeros = (0,) * len(shape)
    return pl.BlockSpec(shape, lambda *_: zeros, pipeline_mode=pl.Buffered(1))


def _dot(a, b):
    return jnp.dot(a, b, preferred_element_type=F32)


def _sigmoid(x):
    return 1.0 / (1.0 + jnp.exp(-x))


def _pack_rows(xb):
    bits = lax.bitcast_convert_type(xb.astype(F32), U32)
    return (bits[:, :HALF] >> 16) | (bits[:, HALF:] & jnp.uint32(0xFFFF0000))


def _unpack_rows_f32(p):
    return (lax.bitcast_convert_type(p << 16, F32),
            lax.bitcast_convert_type(p & jnp.uint32(0xFFFF0000), F32))


def _inproj_kernel(x_ref, gmix_ref, w_ref, wf_ref, bf_ref, gq_ref, gk_ref,
                   u_ref, q_ref, k_ref, v_ref, logf_ref, kb_ref, vb_ref):
    x = x_ref[...]
    ms = jnp.mean(x * x, axis=-1, keepdims=True)
    h = (x * lax.rsqrt(ms + EPS) * gmix_ref[...]).astype(BF16)
    p, a = POOL_WIDTH, ATTN_WIDTH
    u_ref[...] = _dot(h, w_ref[:, :p])
    zq = _dot(h, w_ref[:, p:p + a])
    zk = _dot(h, w_ref[:, p + a:p + 2 * a])
    v = _dot(h, w_ref[:, p + 2 * a:p + 3 * a])
    v_ref[...] = v
    vb_ref[...] = v.astype(BF16)
    scale = HEAD_DIM ** -0.5
    for hd in range(N_HEADS):
        sl = slice(hd * HEAD_DIM, (hd + 1) * HEAD_DIM)
        qh = zq[:, sl]
        qn = qh * lax.rsqrt(jnp.mean(qh * qh, axis=-1, keepdims=True) + EPS) * gq_ref[...]
        q_ref[:, sl] = (qn * scale).astype(BF16)
        kh = zk[:, sl]
        kn = kh * lax.rsqrt(jnp.mean(kh * kh, axis=-1, keepdims=True) + EPS) * gk_ref[...]
        k_ref[:, sl] = kn
        kb_ref[:, sl] = kn.astype(BF16)
    zf = _dot(h, wf_ref[...])[:, :N_HEADS] + bf_ref[...]
    logf_ref[...] = jnp.minimum(zf, 0.0) - jnp.log1p(jnp.exp(-jnp.abs(zf)))


def _inproj(x, g_mix, w_main, w_f, b_f, g_q, g_k):
    t = x.shape[0]
    tm = min(TM_PROJ, t)
    assert t % tm == 0
    row = lambda w: pl.BlockSpec((tm, w), lambda i: (i, 0))
    return pl.pallas_call(
        _inproj_kernel,
        grid=(t // tm,),
        in_specs=[row(D_MODEL), _resident((1, D_MODEL)), _resident(w_main.shape), _resident(w_f.shape),
                  _resident((1, N_HEADS)), _resident((1, HEAD_DIM)), _resident((1, HEAD_DIM))],
        out_specs=[row(POOL_WIDTH), row(ATTN_WIDTH), row(ATTN_WIDTH), row(ATTN_WIDTH), row(N_HEADS),
                   row(ATTN_WIDTH), row(ATTN_WIDTH)],
        out_shape=[jax.ShapeDtypeStruct((t, POOL_WIDTH), F32),
                   jax.ShapeDtypeStruct((t, ATTN_WIDTH), BF16),
                   jax.ShapeDtypeStruct((t, ATTN_WIDTH), F32),
                   jax.ShapeDtypeStruct((t, ATTN_WIDTH), F32),
                   jax.ShapeDtypeStruct((t, N_HEADS), F32),
                   jax.ShapeDtypeStruct((t, ATTN_WIDTH), BF16),
                   jax.ShapeDtypeStruct((t, ATTN_WIDTH), BF16)],
        compiler_params=_params(("arbitrary",)),
        name="inproj",
    )(x, g_mix, w_main, w_f, b_f, g_q, g_k)


def _cumsum_kernel(x_ref, init_ref, o_ref, carry_ref):
    @pl.when(pl.program_id(0) == 0)
    def _():
        carry_ref[...] = init_ref[...]

    x = x_ref[...]
    tl = x.shape[0]
    r = lax.broadcasted_iota(jnp.int32, (tl, tl), 0)
    c = lax.broadcasted_iota(jnp.int32, (tl, tl), 1)
    tri = (r >= c).astype(F32)
    cs = jnp.dot(tri, x, precision=lax.Precision.HIGHEST, preferred_element_type=F32) + carry_ref[...]
    o_ref[...] = cs
    carry_ref[...] = cs[tl - 1:tl, :]


def _cumsum(x, init):
    b, l, h = x.shape
    n = b * h
    tl = min(TL_SCAN, l)
    assert l % tl == 0
    ct = pl.pallas_call(
        _cumsum_kernel,
        grid=(l // tl,),
        in_specs=[pl.BlockSpec((tl, n), lambda li: (li, 0)),
                  pl.BlockSpec((1, n), lambda li: (0, 0))],
        out_specs=pl.BlockSpec((tl, n), lambda li: (li, 0)),
        out_shape=jax.ShapeDtypeStruct((l, n), F32),
        scratch_shapes=[pltpu.VMEM((1, n), F32)],
        compiler_params=_params(("arbitrary",)),
        name="logf_cumsum",
    )(x.transpose(1, 0, 2).reshape(l, n), init.reshape(1, n))
    return ct.reshape(l, b, h).transpose(1, 0, 2)


def _pool_kernel(u_ref, prefix_ref, wp_ref, ps_ref, o_ref, ext_ref, *, start_pos, tl):
    l = pl.program_id(1)

    @pl.when(l == 0)
    def _():
        ext_ref[0:POOL_HALO, :] = prefix_ref[0]

    u = u_ref[0]
    ext_ref[POOL_HALO:POOL_HALO + tl, :] = u
    pos = start_pos + l * tl + lax.broadcasted_iota(jnp.int32, (tl, 1), 0)
    for g, w in enumerate(POOL_WINDOWS):
        cols = slice(g * POOL_GROUP, (g + 1) * POOL_GROUP)
        s = u[:, cols]
        for j in range(1, w):
            s = s + ext_ref[POOL_HALO - j:POOL_HALO - j + tl, cols]
        cnt = jnp.minimum(w, pos + 1).astype(F32)
        d = s / cnt - u[:, cols]
        og = _dot(d.astype(BF16), wp_ref[g]) * ps_ref[:, cols]
        o_ref[0, :, cols] = og.astype(o_ref.dtype)
    ext_ref[0:POOL_HALO, :] = ext_ref[tl:tl + POOL_HALO, :]


def _pool(u, prefix, w_pool, pool_scale, start_pos):
    b, l, p = u.shape
    tl = min(TL_POOL, l)
    assert l % tl == 0 and tl >= POOL_HALO
    return pl.pallas_call(
        functools.partial(_pool_kernel, start_pos=start_pos, tl=tl),
        grid=(b, l // tl),
        in_specs=[pl.BlockSpec((1, tl, p), lambda bi, li: (bi, li, 0)),
                  pl.BlockSpec((1, POOL_HALO, p), lambda bi, li: (bi, 0, 0)),
                  _resident(w_pool.shape), _resident((1, p))],
        out_specs=pl.BlockSpec((1, tl, p), lambda bi, li: (bi, li, 0)),
        out_shape=jax.ShapeDtypeStruct((b, l, p), BF16),
        scratch_shapes=[pltpu.VMEM((tl + POOL_HALO, p), F32)],
        compiler_params=_params(("arbitrary", "arbitrary")),
        name="pool_mix",
    )(u, prefix, w_pool, pool_scale)


def _softmax_step(s, v, m_prev, l_prev, acc_prev):
    m_new = jnp.maximum(m_prev, jnp.max(s, axis=1, keepdims=True))
    alpha = jnp.exp(m_prev - m_new)
    p = jnp.exp(s - m_new)
    l_new = alpha * l_prev + jnp.sum(p, axis=1, keepdims=True)
    acc_new = alpha * acc_prev + _dot(p.astype(BF16), v)
    return m_new, l_new, acc_new


def _qk(q, k):
    return lax.dot_general(q, k, (((1,), (1,)), ((), ())), preferred_element_type=F32)


def _causal(s):
    r = lax.broadcasted_iota(jnp.int32, s.shape, 0)
    c = lax.broadcasted_iota(jnp.int32, s.shape, 1)
    return jnp.where(r >= c, s, NEG)


def _flash_kernel(tab_ref, q_ref, k_ref, v_ref, cq_ref, ck_ref, o_ref, m_sc, l_sc, acc_sc, cq_sc,
                  *, tq, tk, nq, nk, nbh):
    hd = pl.program_id(1)
    i = pl.program_id(2)
    bh = pl.program_id(0) * N_HEADS + hd
    r = tq // tk

    m_sc[...] = jnp.full(m_sc.shape, NEG, F32)
    l_sc[...] = jnp.zeros(l_sc.shape, F32)
    acc_sc[...] = jnp.zeros(acc_sc.shape, F32)
    cq = cq_ref[0]
    lane = lax.broadcasted_iota(jnp.int32, cq.shape, 1)
    cq_sc[...] = jnp.sum(jnp.where(lane == hd, cq, 0.0), axis=1, keepdims=True)
    q = q_ref[0]

    def tile(jj, masked):
        off = pl.multiple_of(jj * tk, tk)
        s = _qk(q, k_ref[0, pl.ds(off, tk), :]) + cq_sc[...] - ck_ref[0, pl.ds(jj, 1), :]
        if masked:
            rr = lax.broadcasted_iota(jnp.int32, s.shape, 0) + i * tq
            cc = lax.broadcasted_iota(jnp.int32, s.shape, 1) + off
            s = jnp.where(rr >= cc, s, NEG)
        m, l, acc = _softmax_step(s, v_ref[0, pl.ds(off, tk), :], m_sc[...], l_sc[...], acc_sc[...])
        m_sc[...] = m
        l_sc[...] = l
        acc_sc[...] = acc

    for d in range(r):
        tile(i * r + d, True)

    bound = tab_ref[0]
    cmax_i = tab_ref[2 + bh * nq + i]
    cmin_base = 2 + nbh * nq + bh * nk

    def live(jj):
        cmin_j = tab_ref[cmin_base + jnp.maximum(jj, 0)]
        return (jj >= 0) & (bound + cmax_i - cmin_j >= -EXP_UNDERFLOW)

    def body(jj):
        tile(jj, False)
        return jj - 1

    lax.while_loop(live, body, i * r - 1)
    o_ref[0] = (acc_sc[...] / l_sc[...]).astype(o_ref.dtype)


def _flash_static_kernel(tab_ref, q_ref, k_ref, v_ref, cq_ref, ck_ref, o_ref, vaug_sc, acc_sc, cqb_sc,
                         *, tq, tk, nq, nk, nbh):
    hd = pl.program_id(1)
    i = pl.program_id(2)
    bh = pl.program_id(0) * N_HEADS + hd
    r = tq // tk

    @pl.when(i == 0)
    def _():
        vaug_sc[:, :HEAD_DIM] = v_ref[0]
        vaug_sc[:, HEAD_DIM:] = jnp.ones((vaug_sc.shape[0], HEAD_DIM), BF16)

    cq = cq_ref[0]
    lane = lax.broadcasted_iota(jnp.int32, cq.shape, 1)
    col = jnp.sum(jnp.where(lane == hd, cq, 0.0), axis=1, keepdims=True) - tab_ref[1]
    cqb_sc[...] = jnp.broadcast_to(col, cqb_sc.shape)
    acc_sc[...] = jnp.zeros(acc_sc.shape, F32)
    q = q_ref[0]

    def tile(jj, masked):
        off = pl.multiple_of(jj * tk, tk)
        s = _qk(q, k_ref[0, pl.ds(off, tk), :])
        ck = ck_ref[0, pl.ds(jj, 1), :]
        parts = []
        for c in range(tk // HEAD_DIM):
            sl = slice(c * HEAD_DIM, (c + 1) * HEAD_DIM)
            sc = s[:, sl] + cqb_sc[...] - ck[:, sl]
            if masked:
                rr = lax.broadcasted_iota(jnp.int32, sc.shape, 0) + i * tq
                cc = lax.broadcasted_iota(jnp.int32, sc.shape, 1) + (off + c * HEAD_DIM)
                sc = jnp.where(rr >= cc, sc, NEG)
            parts.append(jnp.exp(sc).astype(BF16))
        acc_sc[...] += _dot(jnp.concatenate(parts, axis=1), vaug_sc[pl.ds(off, tk), :])

    for d in range(r):
        tile(i * r + d, True)

    cmax_i = tab_ref[2 + bh * nq + i]
    cmin_base = 2 + nbh * nq + bh * nk

    def live(jj):
        cmin_j = tab_ref[cmin_base + jnp.maximum(jj, 0)]
        return (jj >= 0) & (tab_ref[0] + cmax_i - cmin_j >= -EXP_UNDERFLOW)

    def body(jj):
        tile(jj, False)
        return jj - 1

    lax.while_loop(live, body, i * r - 1)
    o_ref[0] = (acc_sc[:, :HEAD_DIM] / acc_sc[:, HEAD_DIM:]).astype(o_ref.dtype)


def _skip_table(c, tq, tk, slack, shift):
    b, s, h = c.shape
    cmax = c.reshape(b, s // tq, tq, h).max(axis=2)
    cmin = lax.cummin(c.reshape(b, s // tk, tk, h).min(axis=2), axis=1)
    flat = lambda a: a.transpose(0, 2, 1).reshape(-1)
    return jnp.concatenate([slack.reshape(1), shift.reshape(1), flat(cmax), flat(cmin)]).astype(F32)


def _prompt_attention(q, k, v, c_col, qk_bound):
    b, s, _ = q.shape
    tq = min(TQ_ATTN, s)
    tk = min(TK_ATTN, tq)
    assert s % tq == 0 and tq % tk == 0 and tk % HEAD_DIM == 0
    nq, nk = s // tq, s // tk
    c_row = c_col.transpose(0, 2, 1).reshape(b * N_HEADS, nk, tk)
    kv_spec = pl.BlockSpec((1, s, HEAD_DIM), lambda bi, hi, i, tab: (bi, 0, hi))
    in_specs = [pl.BlockSpec((1, tq, HEAD_DIM), lambda bi, hi, i, tab: (bi, i, hi)),
                kv_spec, kv_spec,
                pl.BlockSpec((1, tq, N_HEADS), lambda bi, hi, i, tab: (bi, i, 0)),
                pl.BlockSpec((1, nk, tk), lambda bi, hi, i, tab: (bi * N_HEADS + hi, 0, 0))]
    out_spec = pl.BlockSpec((1, tq, HEAD_DIM), lambda bi, hi, i, tab: (bi, i, hi))
    static = dict(tq=tq, tk=tk, nq=nq, nk=nk, nbh=b * N_HEADS)

    def call(body, scratch, tab):
        return pl.pallas_call(
            functools.partial(body, **static),
            grid_spec=pltpu.PrefetchScalarGridSpec(num_scalar_prefetch=1, grid=(b, N_HEADS, nq), in_specs=in_specs,
                                                   out_specs=out_spec, scratch_shapes=scratch),
            out_shape=jax.ShapeDtypeStruct((b, s, ATTN_WIDTH), BF16),
            compiler_params=_params(("parallel", "parallel", "arbitrary")),
            name="prompt_attention",
        )(tab, q, k, v, c_col, c_row)

    def fixed_shift():
        tab = _skip_table(c_col, tq, tk, jnp.zeros((), F32), qk_bound + 1.0)
        return call(_flash_static_kernel,
                    [pltpu.VMEM((s, 2 * HEAD_DIM), BF16), pltpu.VMEM((tq, 2 * HEAD_DIM), F32),
                     pltpu.VMEM((tq, HEAD_DIM), F32)], tab)

    def running_max():
        tab = _skip_table(c_col, tq, tk, 2.0 * qk_bound + 2.0, jnp.zeros((), F32))
        return call(_flash_kernel,
                    [pltpu.VMEM((tq, 1), F32), pltpu.VMEM((tq, 1), F32), pltpu.VMEM((tq, HEAD_DIM), F32),
                     pltpu.VMEM((tq, 1), F32)], tab)

    return lax.cond(2.0 * qk_bound + 1.0 <= STATIC_SOFTMAX_RANGE, fixed_shift, running_max)


def _sample_attn_kernel(q_ref, ck_ref, cv_ref, kn_ref, vn_ref, cq_ref, cc_ref, cn_ref, o_ref,
                        m_sc, l_sc, acc_sc, *, nk, tk):
    j = pl.program_id(1)

    @pl.when(j == 0)
    def _():
        m_sc[...] = jnp.full(m_sc.shape, NEG, F32)
        l_sc[...] = jnp.zeros(l_sc.shape, F32)
        acc_sc[...] = jnp.zeros(acc_sc.shape, F32)

    def step(head_rows, k_ref, v_ref, crow_ref, masked):
        for hd in range(N_HEADS):
            sl = slice(hd * HEAD_DIM, (hd + 1) * HEAD_DIM)
            s = _qk(q_ref[0, :, sl], head_rows(k_ref, hd).astype(BF16))
            s = s + cq_ref[0, :, hd:hd + 1] - crow_ref[0, hd:hd + 1, :]
            if masked:
                s = _causal(s)
            m, l, acc = _softmax_step(s, head_rows(v_ref, hd).astype(BF16), m_sc[hd], l_sc[hd], acc_sc[hd])
            m_sc[hd] = m
            l_sc[hd] = l
            acc_sc[hd] = acc

    cached = lambda ref, hd: ref[0, pl.ds(hd, tk, stride=N_HEADS), :]
    fresh = lambda ref, hd: ref[0, :, hd * HEAD_DIM:(hd + 1) * HEAD_DIM]

    @pl.when(j < nk)
    def _():
        step(cached, ck_ref, cv_ref, cc_ref, False)

    @pl.when(j == nk)
    def _():
        step(fresh, kn_ref, vn_ref, cn_ref, True)
        for hd in range(N_HEADS):
            sl = slice(hd * HEAD_DIM, (hd + 1) * HEAD_DIM)
            o_ref[0, :, sl] = (acc_sc[hd] / l_sc[hd]).astype(o_ref.dtype)


def _sample_attention(q, cache_k, cache_v, k_new, v_new, cq_col, cc_row, cn_row):
    b, l, _ = q.shape
    past = cache_k.shape[1] // N_HEADS
    tk = min(TK_CACHE, past)
    assert past % tk == 0
    nk = past // tk
    cache_spec = pl.BlockSpec((1, tk * N_HEADS, HEAD_DIM), lambda bi, j: (bi, jnp.minimum(j, nk - 1), 0))
    new_spec = pl.BlockSpec((1, l, ATTN_WIDTH), lambda bi, j: (bi, 0, 0))
    return pl.pallas_call(
        functools.partial(_sample_attn_kernel, nk=nk, tk=tk),
        grid=(b, nk + 1),
        in_specs=[new_spec, cache_spec, cache_spec, new_spec, new_spec,
                  pl.BlockSpec((1, l, N_HEADS), lambda bi, j: (bi, 0, 0)),
                  pl.BlockSpec((1, N_HEADS, tk), lambda bi, j: (bi, 0, jnp.minimum(j, nk - 1))),
                  pl.BlockSpec((1, N_HEADS, l), lambda bi, j: (bi, 0, 0))],
        out_specs=new_spec,
        out_shape=jax.ShapeDtypeStruct((b, l, ATTN_WIDTH), BF16),
        scratch_shapes=[pltpu.VMEM((N_HEADS, l, 1), F32), pltpu.VMEM((N_HEADS, l, 1), F32),
                        pltpu.VMEM((N_HEADS, l, HEAD_DIM), F32)],
        compiler_params=_params(("parallel", "arbitrary")),
        name="sample_attention",
    )(q, cache_k, cache_v, k_new, v_new, cq_col, cc_row, cn_row)


def _finish_kernel(x_ref, po_ref, at_ref, wo_ref, gffn_ref, wrh_ref, wrl_ref, rb_ref, wsg_ref, wsu_ref, wsd_ref,
                   cin_ref,
                   base_ref, hp_ref, idx_ref, rank_ref, gate_ref, cnt_ref, carry_sc):
    @pl.when(pl.program_id(0) == 0)
    def _():
        carry_sc[...] = cin_ref[...]

    tm = x_ref.shape[0]
    x1 = x_ref[...] + _dot(po_ref[...], wo_ref[:POOL_WIDTH, :]) + _dot(at_ref[...], wo_ref[POOL_WIDTH:, :])
    ms = jnp.mean(x1 * x1, axis=-1, keepdims=True)
    h = x1 * lax.rsqrt(ms + EPS) * gffn_ref[...]
    hb = h.astype(BF16)

    g = _dot(hb, wsg_ref[...])
    up = _dot(hb, wsu_ref[...])
    act = (g * _sigmoid(g)) * up
    base_ref[...] = x1 + _dot(act.astype(BF16), wsd_ref[...])

    hp_ref[...] = _pack_rows(hb)

    h_lo = (h - hb.astype(F32)).astype(BF16)
    logits = _dot(hb, wrh_ref[...]) + (_dot(h_lo, wrh_ref[...]) + _dot(hb, wrl_ref[...]))
    scores = _sigmoid(logits)
    sel = scores + rb_ref[...]
    lane = lax.broadcasted_iota(jnp.int32, (tm, N_EXPERTS), 1).astype(F32)
    mask = jnp.zeros((tm, N_EXPERTS), F32)
    idx_cols, gate_cols = [], []
    for _ in range(TOP_K):
        mx = jnp.max(sel, axis=1, keepdims=True)
        ik = jnp.min(jnp.where(sel == mx, lane, float(N_EXPERTS)), axis=1, keepdims=True)
        oh = lane == ik
        gate_cols.append(jnp.sum(jnp.where(oh, scores, 0.0), axis=1, keepdims=True))
        idx_cols.append(ik)
        sel = jnp.where(oh, -jnp.inf, sel)
        mask = jnp.where(oh, 1.0, mask)

    r = lax.broadcasted_iota(jnp.int32, (tm, tm), 0)
    c = lax.broadcasted_iota(jnp.int32, (tm, tm), 1)
    tri = (r > c).astype(BF16)
    rank = _dot(tri, mask.astype(BF16)) + carry_sc[...]
    carry_sc[...] = carry_sc[...] + jnp.sum(mask, axis=0, keepdims=True)
    cnt_ref[...] = carry_sc[...]

    gsum = gate_cols[0]
    for gk in gate_cols[1:]:
        gsum = gsum + gk
    lane_k = lax.broadcasted_iota(jnp.int32, (tm, TOP_K), 1)
    idx_out = jnp.zeros((tm, TOP_K), F32)
    rank_out = jnp.zeros((tm, TOP_K), F32)
    gate_out = jnp.zeros((tm, TOP_K), F32)
    for k in range(TOP_K):
        rk = jnp.sum(jnp.where(lane == idx_cols[k], rank, 0.0), axis=1, keepdims=True)
        idx_out = jnp.where(lane_k == k, idx_cols[k], idx_out)
        rank_out = jnp.where(lane_k == k, rk, rank_out)
        gate_out = jnp.where(lane_k == k, gate_cols[k] / gsum * ROUTED_SCALE, gate_out)
    idx_ref[...] = idx_out.astype(jnp.int32)
    rank_ref[...] = rank_out.astype(jnp.int32)
    gate_ref[...] = gate_out


def _finish(x, pool_out, attn, w_o, g_ffn, wr_hi, wr_lo, router_bias, ws_gate, ws_up, ws_down, counts_in):
    t = x.shape[0]
    tm = min(TM_FINISH, t)
    assert t % tm == 0
    row = lambda w: pl.BlockSpec((tm, w), lambda i: (i, 0))
    return pl.pallas_call(
        _finish_kernel,
        grid=(t // tm,),
        in_specs=[row(D_MODEL), row(POOL_WIDTH), row(ATTN_WIDTH), _resident(w_o.shape), _resident((1, D_MODEL)),
                  _resident(wr_hi.shape), _resident(wr_lo.shape), _resident((1, N_EXPERTS)),
                  _resident(ws_gate.shape), _resident(ws_up.shape), _resident(ws_down.shape),
                  _resident((1, N_EXPERTS))],
        out_specs=[row(D_MODEL), row(HALF), row(TOP_K), row(TOP_K), row(TOP_K),
                   pl.BlockSpec((1, N_EXPERTS), lambda i: (0, 0))],
        out_shape=[jax.ShapeDtypeStruct((t, D_MODEL), F32),
                   jax.ShapeDtypeStruct((t, HALF), U32),
                   jax.ShapeDtypeStruct((t, TOP_K), jnp.int32),
                   jax.ShapeDtypeStruct((t, TOP_K), jnp.int32),
                   jax.ShapeDtypeStruct((t, TOP_K), F32),
                   jax.ShapeDtypeStruct((1, N_EXPERTS), F32)],
        scratch_shapes=[pltpu.VMEM((1, N_EXPERTS), F32)],
        compiler_params=_params(("arbitrary",)),
        name="outproj_route",
    )(x, pool_out, attn, w_o, g_ffn, wr_hi, wr_lo, router_bias, ws_gate, ws_up, ws_down, counts_in)


def _dispatch_kernel(dest_ref, hp_ref, *rest):
    xs_ref, sem = rest[-2:]
    tm = hp_ref.shape[0]

    def row_copy(r, d):
        return pltpu.make_async_copy(hp_ref.at[pl.ds(r, 1)], xs_ref.at[pl.ds(d, 1)], sem)

    def drain(r, carry):
        for k in range(TOP_K):
            row_copy(r, dest_ref[r * TOP_K + k]).wait()
        return carry

    for r in range(tm):
        for k in range(TOP_K):
            row_copy(r, dest_ref[r * TOP_K + k]).start(priority=k % 2)
    lax.fori_loop(0, tm, drain, 0)


def _dispatch(dest, hp, xs=None, n_rows=None):
    t = hp.shape[0]
    tm = min(TM_ROWS, t)
    assert t % tm == 0
    in_specs = [pl.BlockSpec((tm * TOP_K,), lambda i: (i,), memory_space=pltpu.SMEM),
                pl.BlockSpec((tm, HALF), lambda i: (i, 0))]
    args = [dest, hp]
    if xs is not None:
        in_specs.append(pl.BlockSpec(memory_space=pl.ANY))
        args.append(xs)
        n_rows = xs.shape[0]
    return pl.pallas_call(
        _dispatch_kernel,
        grid=(t // tm,),
        in_specs=in_specs,
        out_specs=pl.BlockSpec(memory_space=pl.ANY),
        out_shape=jax.ShapeDtypeStruct((n_rows, HALF), U32),
        scratch_shapes=[pltpu.SemaphoreType.DMA],
        input_output_aliases={2: 0} if xs is not None else {},
        compiler_params=_params(("arbitrary",)),
        name="moe_dispatch",
    )(*args)


def _unpack(p):
    lo = lax.bitcast_convert_type(p << 16, F32).astype(BF16)
    hi = lax.bitcast_convert_type(p & jnp.uint32(0xFFFF0000), F32).astype(BF16)
    return lo, hi


def _expert_kernel(blk_e_ref, nblk_ref, valid_ref, first_ref, slot_ref, next_ref,
                   xs_ref, wg_hbm, wu_hbm, wd_hbm, y_ref,
                   wg_f32, wu_f32, wd_f32, wg_sc, wu_sc, wd_sc, sems):
    b = pl.program_id(0)
    slot = slot_ref[b]

    def weight_copies(e, s):
        return (pltpu.make_async_copy(wg_hbm.at[e], wg_f32.at[s], sems.at[s, 0]),
                pltpu.make_async_copy(wu_hbm.at[e], wu_f32.at[s], sems.at[s, 1]),
                pltpu.make_async_copy(wd_hbm.at[e], wd_f32.at[s], sems.at[s, 2]))

    @pl.when(b == 0)
    def _():
        for cp in weight_copies(blk_e_ref[0], 0):
            cp.start()

    @pl.when(first_ref[b] == 1)
    def _():
        for cp in weight_copies(blk_e_ref[b], slot):
            cp.wait()

        @pl.when(next_ref[b] >= 0)
        def _():
            for cp in weight_copies(next_ref[b], 1 - slot):
                cp.start()

        wg_sc[...] = wg_f32[slot].astype(BF16)
        wu_sc[...] = wu_f32[slot].astype(BF16)
        wd_sc[...] = wd_f32[slot].astype(BF16)

    @pl.when(b < nblk_ref[0])
    def _():
        row = lax.broadcasted_iota(jnp.int32, (xs_ref.shape[0], 1), 0)
        lo, hi = _unpack(jnp.where(row < valid_ref[b], xs_ref[...], jnp.uint32(0)))
        g = _dot(lo, wg_sc[:HALF, :]) + _dot(hi, wg_sc[HALF:, :])
        up = _dot(lo, wu_sc[:HALF, :]) + _dot(hi, wu_sc[HALF:, :])
        act = (g * _sigmoid(g)) * up
        y_ref[...] = _pack_rows(_dot(act.astype(BF16), wd_sc[...]).astype(BF16))


def _expert_tables(counts, pends, pstarts, nb, m):
    experts = jnp.arange(N_EXPERTS, dtype=jnp.int32)
    blk_row = jnp.arange(nb, dtype=jnp.int32) * m
    nblk = pends[-1] // m
    blk_e = jnp.minimum(jnp.sum((pends[None, :] <= blk_row[:, None]).astype(jnp.int32), axis=1), N_EXPERTS - 1)
    of_blk = blk_e[:, None] == experts
    pick = lambda table: jnp.sum(jnp.where(of_blk, table, 0), axis=1)
    blk_valid = jnp.clip(pick(counts) - (blk_row - pick(pstarts)), 0, m)
    prev_e = jnp.concatenate([jnp.full((1,), -1, jnp.int32), blk_e[:-1]])
    first = ((blk_row < nblk * m) & (blk_e != prev_e)).astype(jnp.int32)
    slot = (jnp.cumsum(first) - 1) % 2
    later_used = lax.cummin(jnp.where(counts > 0, experts, N_EXPERTS), axis=0, reverse=True)
    next_used = jnp.concatenate([later_used[1:], jnp.full((1,), N_EXPERTS, jnp.int32)])
    next_used = jnp.where(next_used < N_EXPERTS, next_used, -1)
    i32 = lambda a: a.astype(jnp.int32)
    return i32(blk_e), i32(nblk.reshape(1)), i32(blk_valid), first, i32(jnp.maximum(slot, 0)), i32(pick(next_used))


def _experts(tables, xs, w_gate, w_up, w_down):
    nb = tables[0].shape[0]
    m = M_EXPERT
    last = lambda b, nblk_ref: jnp.minimum(b, nblk_ref[0] - 1)
    grid_spec = pltpu.PrefetchScalarGridSpec(
        num_scalar_prefetch=len(tables),
        grid=(nb,),
        in_specs=[pl.BlockSpec((m, HALF), lambda b, be, nr, *_: (last(b, nr), 0)),
                  pl.BlockSpec(memory_space=pl.ANY), pl.BlockSpec(memory_space=pl.ANY),
                  pl.BlockSpec(memory_space=pl.ANY)],
        out_specs=pl.BlockSpec((m, HALF), lambda b, be, nr, *_: (last(b, nr), 0)),
        scratch_shapes=[pltpu.VMEM((2, D_MODEL, D_EXPERT), F32), pltpu.VMEM((2, D_MODEL, D_EXPERT), F32),
                        pltpu.VMEM((2, D_EXPERT, D_MODEL), F32),
                        pltpu.VMEM((D_MODEL, D_EXPERT), BF16), pltpu.VMEM((D_MODEL, D_EXPERT), BF16),
                        pltpu.VMEM((D_EXPERT, D_MODEL), BF16),
                        pltpu.SemaphoreType.DMA((2, 3))],
    )
    return pl.pallas_call(
        _expert_kernel,
        grid_spec=grid_spec,
        out_shape=jax.ShapeDtypeStruct((nb * m, HALF), U32),
        compiler_params=_params(("arbitrary",)),
        name="moe_experts",
    )(*tables, xs, w_gate, w_up, w_down)


def _combine_kernel(dest_ref, dest_next_ref, base_ref, gate_ref, ys_ref, o_ref, buf, sems):
    tm = base_ref.shape[0]
    i = pl.program_id(0)

    def row_copy(d_ref, s, r, k):
        return pltpu.make_async_copy(ys_ref.at[pl.ds(d_ref[r * TOP_K + k], 1)], buf.at[s, k, pl.ds(r, 1)],
                                     sems.at[s])

    def issue_all(d_ref, s):
        for r in range(tm):
            for k in range(TOP_K):
                row_copy(d_ref, s, r, k).start(priority=k % 2)

    def step(s):
        @pl.when(i + 1 < pl.num_programs(0))
        def _():
            issue_all(dest_next_ref, 1 - s)

        def drain(r, carry):
            for k in range(TOP_K):
                row_copy(dest_ref, s, r, k).wait()
            return carry

        lax.fori_loop(0, tm, drain, 0)
        for r0 in range(0, tm, COMBINE_ROWS):
            rows = slice(r0, r0 + COMBINE_ROWS)
            acc_lo = base_ref[rows, :HALF]
            acc_hi = base_ref[rows, HALF:]
            for k in range(TOP_K):
                g = gate_ref[rows, k:k + 1]
                lo, hi = _unpack_rows_f32(buf[s, k, rows, :])
                acc_lo = acc_lo + g * lo
                acc_hi = acc_hi + g * hi
            o_ref[rows, :HALF] = acc_lo
            o_ref[rows, HALF:] = acc_hi

    @pl.when(i == 0)
    def _():
        issue_all(dest_ref, 0)

    for s in range(2):
        pl.when(i % 2 == s)(functools.partial(step, s))


def _combine(dest, base, gates, ys):
    t = base.shape[0]
    tm = min(TM_ROWS, t)
    assert t % tm == 0
    n = t // tm
    return pl.pallas_call(
        _combine_kernel,
        grid=(n,),
        in_specs=[pl.BlockSpec((tm * TOP_K,), lambda i: (i,), memory_space=pltpu.SMEM),
                  pl.BlockSpec((tm * TOP_K,), lambda i: (jnp.minimum(i + 1, n - 1),), memory_space=pltpu.SMEM),
                  pl.BlockSpec((tm, D_MODEL), lambda i: (i, 0)),
                  pl.BlockSpec((tm, TOP_K), lambda i: (i, 0)),
                  pl.BlockSpec(memory_space=pl.ANY)],
        out_specs=pl.BlockSpec((tm, D_MODEL), lambda i: (i, 0)),
        out_shape=jax.ShapeDtypeStruct((t, D_MODEL), F32),
        scratch_shapes=[pltpu.VMEM((2, TOP_K, tm, HALF), U32), pltpu.SemaphoreType.DMA((2,))],
        compiler_params=_params(("arbitrary",)),
        name="moe_combine",
    )(dest, dest, base, gates, ys)


def _mixer(x, prefix, start_pos, w, init_c):
    b, l, _ = x.shape
    u, q, k, v, logf, kb, vb = _inproj(x.reshape(b * l, D_MODEL), w["g_mix"], w["w_main"], w["w_f"], w["b_f"],
                                       w["g_q"], w["g_k"])
    u = u.reshape(b, l, POOL_WIDTH)
    pool_out = _pool(u, prefix, w["w_pool"], w["pool_scale"], start_pos)
    c = _cumsum(logf.reshape(b, l, N_HEADS), init_c)
    shp = (b, l, ATTN_WIDTH)
    return u, q.reshape(shp), k.reshape(shp), v.reshape(shp), logf, pool_out, c, kb.reshape(shp), vb.reshape(shp)


def _slot_starts(idx, pstarts):
    experts = jnp.arange(N_EXPERTS, dtype=jnp.int32)
    return jnp.sum(jnp.where(idx[..., None] == experts, pstarts, 0), axis=-1)


def kernel(x_prompt, x_sample, cache_k, cache_v, cache_logf, cache_pool, g_mix, w_in, b_f, g_q, g_k, w_pool, pool_scale, w_o, g_ffn, w_router, router_bias, w_gate, w_up, w_down, ws_gate, ws_up, ws_down):
    depth = w_in.shape[0]
    assert depth == 1, "single-layer step"
    bp, sp, _ = x_prompt.shape
    bs, ls, _ = x_sample.shape
    past = cache_k.shape[2]
    assert sp >= POOL_BUF and ls >= POOL_BUF
    n_main = POOL_WIDTH + 3 * ATTN_WIDTH

    w = dict(
        g_mix=g_mix[0][None], g_q=g_q[0][None], g_k=g_k[0][None], b_f=b_f[0][None],
        w_main=w_in[0, :, :n_main].astype(BF16),
        w_f=jnp.pad(w_in[0, :, n_main:], ((0, 0), (0, HEAD_DIM - N_HEADS))).astype(BF16),
        w_pool=w_pool[0].astype(BF16), pool_scale=pool_scale[0][None],
    )
    wr_hi = w_router[0].astype(BF16)
    wr_lo = (w_router[0] - wr_hi.astype(F32)).astype(BF16)
    fin = (w_o[0].astype(BF16), g_ffn[0][None], wr_hi, wr_lo, router_bias[0][None],
           ws_gate[0].astype(BF16), ws_up[0].astype(BF16), ws_down[0].astype(BF16))
    qk_bound = 1.02 * HEAD_DIM ** 0.5 * jnp.max(jnp.abs(g_q[0])) * jnp.max(jnp.abs(g_k[0]))

    zero_c = lambda b: jnp.zeros((b, 1, N_HEADS), F32)

    u_p, q_p, k_p, v_p, logf_p, pool_p, c_p, kb_p, vb_p = _mixer(
        x_prompt, jnp.zeros((bp, POOL_HALO, POOL_WIDTH), F32), 0, w, zero_c(bp))
    attn_p = _prompt_attention(q_p, kb_p, vb_p, c_p, qk_bound)

    ck = cache_k[0].reshape(bs, past * N_HEADS, HEAD_DIM)
    cv = cache_v[0].reshape(bs, past * N_HEADS, HEAD_DIM)
    c_cache = _cumsum(cache_logf[0], zero_c(bs))
    prefix_s = jnp.pad(cache_pool[0], ((0, 0), (POOL_HALO - POOL_BUF, 0), (0, 0)))
    u_s, q_s, k_s, v_s, logf_s, pool_s, c_s, _, _ = _mixer(x_sample, prefix_s, past, w,
                                                           c_cache[:, past - 1:past, :])
    attn_s = _sample_attention(q_s, ck, cv, k_s, v_s, c_s, c_cache.transpose(0, 2, 1), c_s.transpose(0, 2, 1))

    tp, ts = bp * sp, bs * ls
    base_p, hp_p, idx_p, rank_p, gate_p, cnt_p = _finish(
        x_prompt.reshape(tp, D_MODEL), pool_p.reshape(tp, POOL_WIDTH), attn_p.reshape(tp, ATTN_WIDTH), *fin,
        jnp.zeros((1, N_EXPERTS), F32))
    base_s, hp_s, idx_s, rank_s, gate_s, cnt_s = _finish(
        x_sample.reshape(ts, D_MODEL), pool_s.reshape(ts, POOL_WIDTH), attn_s.reshape(ts, ATTN_WIDTH), *fin, cnt_p)

    m = M_EXPERT
    counts = cnt_s[0].astype(jnp.int32)
    pcounts = (counts + m - 1) // m * m
    pends = jnp.cumsum(pcounts)
    pstarts = pends - pcounts
    nb = -(-(tp + ts) * TOP_K // m) + N_EXPERTS
    tables = _expert_tables(counts, pends, pstarts, nb, m)
    dest_p = (_slot_starts(idx_p, pstarts) + rank_p).reshape(tp * TOP_K)
    dest_s = (_slot_starts(idx_s, pstarts) + rank_s).reshape(ts * TOP_K)

    xs = _dispatch(dest_p, hp_p, n_rows=nb * m)
    xs = _dispatch(dest_s, hp_s, xs)
    ys = _experts(tables, xs, w_gate[0], w_up[0], w_down[0])
    y_p = _combine(dest_p, base_p, gate_p, ys).reshape(bp, sp, D_MODEL)
    y_s = _combine(dest_s, base_s, gate_s, ys).reshape(bs, ls, D_MODEL)

    heads = lambda a, b, l: a.reshape(1, b, l, N_HEADS, HEAD_DIM)
    return (y_p, y_s,
            heads(k_p, bp, sp), heads(v_p, bp, sp), logf_p.reshape(1, bp, sp, N_HEADS), u_p[None, :, sp - POOL_BUF:, :],
            heads(k_s, bs, ls), heads(v_s, bs, ls), logf_s.reshape(1, bs, ls, N_HEADS), u_s[None, :, ls - POOL_BUF:, :])
```
